```python
import math
import jax
import jax.numpy as jnp
from jax import lax
import numpy as np

D_MODEL = 1024
BATCH = 8
SEQ = 2048
DEPTH = 4
DEC_BATCH = 32
DEC_SEQ = 8
PAST_LEN = 16384
PAGE_SIZE = 128

DH = 64
H_A = 4
H_IDX = 4
D_IDX = 32
TOPK_A_MAX = 256
H_B = 4
G_B = 2
MOBA_BLOCK = 256
MOBA_TOPK = 3
H_C = 4
Q_LORA = 192
KV_LORA = 128
D_NOPE = 64
D_ROPE = 32
D_V_C = 64
ROPE_THETA = 10000.0
H_D = 4
DK_D = 64
DV_D = 64
CHUNK_D = 16
N_BRANCH = 4
BRANCH_W = 256
N_BUCKETS = 32
MAX_DIST = 128
N_KEYS = 128
N_EXPERTS = N_KEYS * N_KEYS
PEER_HEADS = 8
D_KEY = 256
PEER_TOPK = 16
PEER_BLOCK = 256
D_PLE = 256
QBLOCK = 128
MOBA_QBLOCK = 8
EPS = 1e-6
NEG = -1e30
F_MIN = 1e-20
ROW_A = 2 * DH + D_IDX
ROW_C = KV_LORA + D_ROPE
PROJ_WIDTHS = (H_A * DH, DH, DH, H_IDX * D_IDX, H_IDX, D_IDX,
               H_B * DH, G_B * DH, G_B * DH,
               Q_LORA, KV_LORA, D_ROPE,
               H_D * DK_D, H_D * DK_D, H_D * DV_D, H_D * DV_D,
               N_BRANCH * D_MODEL)
D_IN = sum(PROJ_WIDTHS)

kernel_name = 'hybrid_dsa_moba_mla_hgrn2_peer_step'


def rmsnorm(x, g):
    xf = x.astype(jnp.float32)
    y = xf * lax.rsqrt(jnp.mean(xf * xf, axis=-1, keepdims=True) + EPS)
    return (y * g.astype(jnp.float32)).astype(x.dtype)


def t5_bucket(rel):
    n = jnp.maximum(rel, 0)
    exact = N_BUCKETS // 2
    nf = jnp.maximum(n, exact).astype(jnp.float32)
    big = exact + (jnp.log(nf / exact) / math.log(MAX_DIST / exact) * (N_BUCKETS - exact)).astype(jnp.int32)
    return jnp.where(n < exact, n, jnp.minimum(big, N_BUCKETS - 1))


def rope(x, pos):
    half = D_ROPE // 2
    inv = 1.0 / (ROPE_THETA ** (jnp.arange(half, dtype=jnp.float32) / half))
    ang = pos.astype(jnp.float32)[:, None] * inv
    ang = ang.reshape(ang.shape[:1] + (1,) * (x.ndim - 3) + (half,))
    cos, sin = jnp.cos(ang), jnp.sin(ang)
    x1 = x[..., :half].astype(jnp.float32)
    x2 = x[..., half:].astype(jnp.float32)
    return jnp.concatenate([x1 * cos - x2 * sin, x1 * sin + x2 * cos], axis=-1).astype(x.dtype)


def gather_pages(pool, page_table):
    g = pool[page_table]
    return g.reshape((g.shape[0], g.shape[1] * g.shape[2]) + g.shape[3:])


def split_points():
    return np.cumsum(np.array(PROJ_WIDTHS))[:-1].tolist()


def query_blocks(fn, block, *xs):
    B, T = xs[0].shape[:2]
    qb = math.gcd(T, block)
    nb = T // qb
    blk = tuple(jnp.moveaxis(a.reshape((B, nb, qb) + a.shape[2:]), 1, 0) for a in xs)
    out = lax.map(lambda args: fn(*args), blk)
    out = jnp.moveaxis(out, 0, 1)
    return out.reshape((B, T) + out.shape[3:])


def dsa_attention(q, iq, iw, k_all, v_all, ik_all, q_pos, tab):
    L = k_all.shape[1]
    topk = min(TOPK_A_MAX, L // 4)
    k_pos = jnp.arange(L, dtype=jnp.int32)
    idx_scale = (H_IDX * D_IDX) ** -0.5
    gather = jax.vmap(lambda a, i: a[i])

    def block(qb, iqb, iwb, pb):
        rel = jax.nn.relu(jnp.einsum('bthc,bsc->bths', iqb, ik_all).astype(jnp.float32))
        score = jnp.einsum('bth,bths->bts', iwb.astype(jnp.float32), rel) * idx_scale
        score = jnp.where(k_pos[None, None, :] <= pb[:, :, None], score, NEG)
        _, sel = lax.top_k(score, topk)
        kg = gather(k_all, sel)
        vg = gather(v_all, sel)
        logits = jnp.einsum('bthd,btkd->bhtk', qb, kg).astype(jnp.float32) * DH ** -0.5
        bias = jnp.moveaxis(tab[t5_bucket(pb[:, :, None] - sel)], -1, 1).astype(jnp.float32)
        valid = (sel <= pb[:, :, None])[:, None]
        logits = jnp.where(valid, logits + bias, NEG)
        p = jax.nn.softmax(logits, axis=-1)
        return jnp.einsum('bhtk,btkd->bthd', p.astype(vg.dtype), vg)

    return query_blocks(block, QBLOCK, q, iq, iw, q_pos)


def moba_attention(q, k_all, v_all, q_pos, tab):
    B, L = k_all.shape[:2]
    nblk = -(-L // MOBA_BLOCK)
    pad = ((0, 0), (0, nblk * MOBA_BLOCK - L), (0, 0), (0, 0))

    def blocks_of(a):
        return jnp.pad(a, pad).reshape(B, nblk, MOBA_BLOCK, G_B, DH).transpose(0, 3, 1, 2, 4)

    kb = blocks_of(k_all)
    vb = blocks_of(v_all)
    grp = jnp.arange(H_B) // (H_B // G_B)
    kmean = jnp.mean(kb.astype(jnp.float32), axis=3)[:, grp]
    ntop = min(MOBA_TOPK, nblk)
    tab_t = tab.T.astype(jnp.float32)
    bidx = jnp.arange(B)[:, None, None, None]
    gidx = grp[None, :, None, None]
    hidx = jnp.arange(H_B)[None, :, None, None, None]
    offs = jnp.arange(MOBA_BLOCK, dtype=jnp.int32)

    def block(qb, pb):
        qh = jnp.swapaxes(qb, 1, 2)
        own = pb // MOBA_BLOCK
        gate = jnp.einsum('bhtd,bhnd->bhtn', qh.astype(jnp.float32), kmean)
        fully_past = jnp.arange(nblk)[None, None, None, :] < own[:, None, :, None]
        gate = jnp.where(fully_past, gate, NEG)
        gval, sel = lax.top_k(gate, ntop)
        own_b = jnp.broadcast_to(own[:, None, :, None], sel.shape[:3] + (1,))
        sel_ok = gval > 0.5 * NEG
        chosen = jnp.concatenate([jnp.where(sel_ok, sel, own_b), own_b], axis=-1)
        chosen_ok = jnp.concatenate([sel_ok, jnp.ones_like(own_b, dtype=bool)], axis=-1)
        kg = kb[bidx, gidx, chosen]
        vg = vb[bidx, gidx, chosen]
        kpos = chosen[..., None] * MOBA_BLOCK + offs
        tq = pb[:, None, :, None, None]
        valid = chosen_ok[..., None] & (kpos <= tq)
        logits = jnp.einsum('bhtd,bhtnkd->bhtnk', qh, kg).astype(jnp.float32) * DH ** -0.5
        logits = logits + tab_t[hidx, t5_bucket(tq - kpos)]
        logits = jnp.where(valid, logits, NEG)
        shp = logits.shape
        p = jax.nn.softmax(logits.reshape(shp[:3] + (-1,)), axis=-1).reshape(shp)
        return jnp.einsum('bhtnk,bhtnkd->bthd', p.astype(vg.dtype), vg)

    return query_blocks(block, MOBA_QBLOCK, q, q_pos)


def mla_attention(q_nope, q_rope, ckv_all, kr_all, q_pos, w_uk, w_uv):
    L = ckv_all.shape[1]
    k_pos = jnp.arange(L, dtype=jnp.int32)
    scale = (D_NOPE + D_ROPE) ** -0.5
    q_lat = jnp.einsum('bthn,chn->bthc', q_nope, w_uk)

    def block(ql, qr, pb):
        s = (jnp.einsum('bthc,bsc->bhts', ql, ckv_all)
             + jnp.einsum('bthr,bsr->bhts', qr, kr_all)).astype(jnp.float32) * scale
        s = jnp.where(k_pos[None, None, None, :] <= pb[:, None, :, None], s, NEG)
        p = jax.nn.softmax(s, axis=-1)
        return jnp.einsum('bhts,bsc->bthc', p.astype(ckv_all.dtype), ckv_all)

    o_lat = query_blocks(block, QBLOCK, q_lat, q_rope, q_pos)
    return jnp.einsum('bthc,chv->bthv', o_lat, w_uv)


def hgrn2_recurrence(q, k, logf, v, s0):
    B, T, H, _ = q.shape
    C = math.gcd(T, CHUNK_D)
    n = T // C

    def chunks(a):
        return jnp.moveaxis(a.reshape((B, n, C) + a.shape[2:]), 1, 0)

    tril = jnp.tril(jnp.ones((C, C), dtype=bool))[None, :, :, None, None]

    def step(S, inp):
        qc, kc, gc, vc = inp
        b = jnp.cumsum(gc, axis=1)
        diff = jnp.where(tril, b[:, :, None] - b[:, None, :], 0.0)
        decay = jnp.where(tril, jnp.exp(diff), 0.0)
        A = jnp.einsum('bthk,btshk,bshk->bhts', qc, decay, kc)
        o = jnp.einsum('bthk,bhkv->bthv', qc * jnp.exp(b), S) + jnp.einsum('bhts,bshv->bthv', A, vc)
        b_last = b[:, -1]
        S = jnp.exp(b_last)[..., None] * S + jnp.einsum('bshk,bshv->bhkv', kc * jnp.exp(b_last[:, None] - b), vc)
        return S, o

    S, o = lax.scan(step, s0, (chunks(q), chunks(k), chunks(logf), chunks(v)))
    return jnp.moveaxis(o, 0, 1).reshape(B, T, H, v.shape[-1]), S


def peer(h, wq, subkeys, u, v):
    B, T, D = h.shape
    n = B * T
    blk = min(PEER_BLOCK, n)
    nb = -(-n // blk)
    flat = jnp.pad(h.reshape(n, D), ((0, nb * blk - n), (0, 0))).reshape(nb, blk, D)

    def block(hb):
        q = (hb @ wq).reshape(blk, PEER_HEADS, 2, D_KEY // 2)
        s = jnp.einsum('nhpc,hpkc->nhpk', q, subkeys).astype(jnp.float32)
        sv, si = lax.top_k(s, PEER_TOPK)
        cand = (sv[:, :, 0, :, None] + sv[:, :, 1, None, :]).reshape(blk, PEER_HEADS, -1)
        cidx = (si[:, :, 0, :, None] * N_KEYS + si[:, :, 1, None, :]).reshape(blk, PEER_HEADS, -1)
        top_s, top_j = lax.top_k(cand, PEER_TOPK)
        eidx = jnp.take_along_axis(cidx, top_j, axis=-1)
        g = jax.nn.softmax(top_s, axis=-1)
        ue = u[eidx]
        ve = v[eidx]
        act = jax.nn.gelu(jnp.einsum('nhkd,nd->nhk', ue, hb).astype(jnp.float32))
        return jnp.einsum('nhk,nhkd->nd', (g * act).astype(ve.dtype), ve)

    out = lax.map(block, flat).reshape(nb * blk, D)[:n]
    return out.reshape(B, T, D)


def decoder_layer(x, p_l, pos0, a_past, b_past, c_past, s0, lb, t5_bias,
                  norm1_g, w_in, cq_norm_g, w_uq, ckv_norm_g, w_uk, w_uv, d_norm_g,
                  w_branch, w_out, norm2_g, peer_wq, peer_subkeys, peer_u, peer_v, ple_gate, ple_proj):
    B, T, _ = x.shape
    f32 = jnp.float32
    pos = pos0 + jnp.arange(T, dtype=jnp.int32)
    pos_b = jnp.broadcast_to(pos, (B, T))
    h = rmsnorm(x, norm1_g)
    (a_q, a_k, a_v, a_iq, a_iw, a_ik, b_q, b_k, b_v, c_q, c_kv, c_kr,
     d_q, d_f, d_i, d_g, gate) = jnp.split(h @ w_in, split_points(), axis=-1)

    rows_a = jnp.concatenate([a_k, a_v, a_ik], axis=-1)
    all_a = jnp.concatenate([a_past.astype(x.dtype), rows_a], axis=1)
    o_a = dsa_attention(a_q.reshape(B, T, H_A, DH), a_iq.reshape(B, T, H_IDX, D_IDX), a_iw,
                        all_a[..., :DH], all_a[..., DH:2 * DH], all_a[..., 2 * DH:], pos_b, t5_bias[:, :H_A])

    rows_b = jnp.stack([b_k.reshape(B, T, G_B, DH), b_v.reshape(B, T, G_B, DH)], axis=2)
    all_b = jnp.concatenate([b_past.astype(x.dtype), rows_b], axis=1)
    o_b = moba_attention(b_q.reshape(B, T, H_B, DH), all_b[:, :, 0], all_b[:, :, 1], pos_b, t5_bias[:, H_A:])

    qc = (rmsnorm(c_q, cq_norm_g) @ w_uq).reshape(B, T, H_C, D_NOPE + D_ROPE)
    rows_c = jnp.concatenate([rmsnorm(c_kv, ckv_norm_g), rope(c_kr, pos)], axis=-1)
    all_c = jnp.concatenate([c_past.astype(x.dtype), rows_c], axis=1)
    o_c = mla_attention(qc[..., :D_NOPE], rope(qc[..., D_NOPE:], pos), all_c[..., :KV_LORA], all_c[..., KV_LORA:],
                        pos_b, w_uk, w_uv)

    lbv = lb.reshape(H_D, DK_D)
    sig_f = jax.nn.sigmoid(d_f.reshape(B, T, H_D, DK_D).astype(f32))
    f_gate = lbv + (1.0 - lbv) * sig_f
    log_f = jnp.log(jnp.maximum(f_gate, F_MIN))
    k_d = (1.0 - lbv) * (1.0 - sig_f)
    q_d = jax.nn.silu(d_q.reshape(B, T, H_D, DK_D).astype(f32))
    o_d, s_new = hgrn2_recurrence(q_d, k_d, log_f, d_i.reshape(B, T, H_D, DV_D).astype(f32), s0.astype(f32))
    o_d = (rmsnorm(o_d, d_norm_g) * jax.nn.silu(d_g.reshape(B, T, H_D, DV_D).astype(f32))).astype(x.dtype)

    branches = jnp.stack([o_a.reshape(B, T, BRANCH_W), o_b.reshape(B, T, BRANCH_W),
                          o_c.reshape(B, T, BRANCH_W), o_d.reshape(B, T, BRANCH_W)], axis=2)
    proj = jnp.einsum('btmw,mwd->btmd', branches, w_branch)
    gates = jax.nn.sigmoid(gate.reshape(B, T, N_BRANCH, D_MODEL))
    x = x + jnp.sum(gates * proj, axis=2) @ w_out

    x = x + peer(rmsnorm(x, norm2_g), peer_wq, peer_subkeys, peer_u, peer_v)
    x = x + jax.nn.sigmoid(x @ ple_gate) * (p_l @ ple_proj)
    return x, (rows_a, rows_b, rows_c, s_new)


def trunk(x, p, pos0, get_past, lb_all, t5_bias, norm1_g, w_in, cq_norm_g, w_uq, ckv_norm_g, w_uk, w_uv,
          d_norm_g, w_branch, w_out, norm2_g, peer_wq, peer_subkeys, peer_u, peer_v, ple_gate, ple_proj,
          final_norm_g):
    rows = []
    for li in range(DEPTH):
        a_past, b_past, c_past, s0 = get_past(li)
        x, new = decoder_layer(x, p[li], pos0, a_past, b_past, c_past, s0, lb_all[li], t5_bias,
                               norm1_g[li], w_in[li], cq_norm_g[li], w_uq[li], ckv_norm_g[li], w_uk[li], w_uv[li],
                               d_norm_g[li], w_branch[li], w_out[li], norm2_g[li], peer_wq[li], peer_subkeys[li],
                               peer_u[li], peer_v[li], ple_gate[li], ple_proj[li])
        rows.append(new)
    y = rmsnorm(x, final_norm_g)
    stacked = tuple(jnp.stack([r[j] for r in rows]) for j in range(4))
    return y, stacked


def setup_inputs(seed: int = 0) -> dict:
    key = jax.random.key(seed)
    keys = iter(list(jax.random.split(key, 40)))

    def nrm(shape, scale):
        return scale * jax.random.normal(next(keys), shape, jnp.float32)

    def gain(shape):
        return 1.0 + 0.05 * jax.random.normal(next(keys), shape, jnp.float32)

    n_pages = PAST_LEN // PAGE_SIZE
    n_used = DEC_BATCH * n_pages
    n_pool = n_used + max(1, n_used // 4)
    page_table = jax.random.permutation(next(keys), n_pool)[:n_used].reshape(DEC_BATCH, n_pages).astype(jnp.int32)
    return {
        'x_prompt': nrm((BATCH, SEQ, D_MODEL), 1.0),
        'x_sample': nrm((DEC_BATCH, DEC_SEQ, D_MODEL), 1.0),
        'cache_a': nrm((DEPTH, n_pool, PAGE_SIZE, ROW_A), 1.0),
        'cache_b': nrm((DEPTH, n_pool, PAGE_SIZE, 2, G_B, DH), 1.0),
        'cache_c': nrm((DEPTH, n_pool, PAGE_SIZE, ROW_C), 1.0),
        'state_d': nrm((DEPTH, DEC_BATCH, H_D, DK_D, DV_D), 0.5),
        'page_table': page_table,
        'p_prompt': nrm((DEPTH, BATCH, SEQ, D_PLE), 1.0),
        'p_sample': nrm((DEPTH, DEC_BATCH, DEC_SEQ, D_PLE), 1.0),
        'norm1_g': gain((DEPTH, D_MODEL)),
        'w_in': nrm((DEPTH, D_MODEL, D_IN), D_MODEL ** -0.5),
        'cq_norm_g': gain((DEPTH, Q_LORA)),
        'w_uq': nrm((DEPTH, Q_LORA, H_C * (D_NOPE + D_ROPE)), Q_LORA ** -0.5),
        'ckv_norm_g': gain((DEPTH, KV_LORA)),
        'w_uk': nrm((DEPTH, KV_LORA, H_C, D_NOPE), KV_LORA ** -0.5),
        'w_uv': nrm((DEPTH, KV_LORA, H_C, D_V_C), KV_LORA ** -0.5),
        'lb_logits': nrm((DEPTH, H_D * DK_D), 1.0),
        'd_norm_g': gain((DEPTH, DV_D)),
        't5_bias': nrm((N_BUCKETS, H_A + H_B), 0.5),
        'w_branch': nrm((DEPTH, N_BRANCH, BRANCH_W, D_MODEL), BRANCH_W ** -0.5),
        'w_out': nrm((DEPTH, D_MODEL, D_MODEL), 0.5 * D_MODEL ** -0.5),
        'norm2_g': gain((DEPTH, D_MODEL)),
        'peer_wq': nrm((DEPTH, D_MODEL, PEER_HEADS * D_KEY), D_MODEL ** -0.5),
        'peer_subkeys': nrm((DEPTH, PEER_HEADS, 2, N_KEYS, D_KEY // 2), (D_KEY // 2) ** -0.5),
        'peer_u': nrm((DEPTH, N_EXPERTS, D_MODEL), D_MODEL ** -0.5),
        'peer_v': nrm((DEPTH, N_EXPERTS, D_MODEL), 0.1),
        'ple_gate': nrm((DEPTH, D_MODEL, D_MODEL), D_MODEL ** -0.5),
        'ple_proj': nrm((DEPTH, D_PLE, D_MODEL), D_PLE ** -0.5),
        'final_norm_g': gain((D_MODEL,)),
    }


def reference(x_prompt, x_sample, cache_a, cache_b, cache_c, state_d, page_table, p_prompt, p_sample,
              norm1_g, w_in, cq_norm_g, w_uq, ckv_norm_g, w_uk, w_uv, lb_logits, d_norm_g, t5_bias,
              w_branch, w_out, norm2_g, peer_wq, peer_subkeys, peer_u, peer_v, ple_gate, ple_proj, final_norm_g):
    sm = jax.nn.softmax(lb_logits.astype(jnp.float32), axis=0)
    lb_all = jnp.maximum(jnp.cumsum(sm, axis=0) - sm[0], 0.0)
    weights = (t5_bias, norm1_g, w_in, cq_norm_g, w_uq, ckv_norm_g, w_uk, w_uv, d_norm_g, w_branch, w_out,
               norm2_g, peer_wq, peer_subkeys, peer_u, peer_v, ple_gate, ple_proj, final_norm_g)
    bp, dt = x_prompt.shape[0], x_prompt.dtype

    def prompt_past(li):
        return (jnp.zeros((bp, 0, ROW_A), dt), jnp.zeros((bp, 0, 2, G_B, DH), dt),
                jnp.zeros((bp, 0, ROW_C), dt), jnp.zeros((bp, H_D, DK_D, DV_D), jnp.float32))

    def sample_past(li):
        return (gather_pages(cache_a[li], page_table), gather_pages(cache_b[li], page_table),
                gather_pages(cache_c[li], page_table), state_d[li])

    past_len = page_table.shape[1] * cache_a.shape[2]
    y_prompt, (na_p, nb_p, nc_p, nd_p) = trunk(x_prompt, p_prompt, 0, prompt_past, lb_all, *weights)
    y_sample, (na_s, nb_s, nc_s, nd_s) = trunk(x_sample, p_sample, past_len, sample_past, lb_all, *weights)
    return (y_prompt, y_sample, na_p, nb_p, nc_p, nd_p, na_s, nb_s, nc_s, nd_s)
```

```python
import functools
import math

import jax
import jax.numpy as jnp
import numpy as np
from jax import lax
from jax.experimental import pallas as pl
from jax.experimental.pallas import tpu as pltpu

F32 = jnp.float32
BF16 = jnp.bfloat16
I32 = jnp.int32

DH = 64
H_A = 4
H_IDX = 4
D_IDX = 32
TOPK_A_MAX = 256
H_B = 4
G_B = 2
MOBA_BLOCK = 256
MOBA_TOPK = 3
H_C = 4
Q_LORA = 192
KV_LORA = 128
D_NOPE = 64
D_ROPE = 32
D_V_C = 64
ROPE_THETA = 10000.0
H_D = 4
DK_D = 64
DV_D = 64
CHUNK_D = 16
N_BRANCH = 4
BRANCH_W = 256
N_BUCKETS = 32
MAX_DIST = 128
N_KEYS = 128
PEER_HEADS = 8
PEER_TOPK = 16
EPS = 1e-6
NEG = -1e30
F_MIN = 1e-20
ROW_A = 2 * DH + D_IDX
ROW_C = KV_LORA + D_ROPE

LANES = 128
VMEM_LIMIT = 56 * 1024 * 1024
INT_MIN = -(2 ** 31)

COL_D = 0
COL_AQ = 1024
COL_AKV = 1280
COL_BQ = 1536
COL_BKV = 1792
COL_CQ = 2048
COL_AIQ = 2304
COL_CKV = 2432
COL_CKR = 2560
P_COLS = 2688


def _cparams(*sem):
    return pltpu.CompilerParams(dimension_semantics=sem, vmem_limit_bytes=VMEM_LIMIT)


def _dot(a, b):
    return jnp.dot(a.astype(BF16), b.astype(BF16), preferred_element_type=F32)


def _dot_nt(a, b):
    return lax.dot_general(a.astype(BF16), b.astype(BF16), (((1,), (1,)), ((), ())),
                           preferred_element_type=F32)


def _split(a):
    hi = a.astype(BF16)
    lo = (a - hi.astype(F32)).astype(BF16)
    return hi, lo


def _dot3_nt(a, b):
    ah, al = _split(a)
    bh, bl = _split(b)
    dn = (((1,), (1,)), ((), ()))
    d = lambda x, y: lax.dot_general(x, y, dn, preferred_element_type=F32)
    return d(ah, bh) + (d(ah, bl) + d(al, bh))


def _rms(x, g):
    return x * lax.rsqrt(jnp.mean(x * x, axis=-1, keepdims=True) + EPS) * g


def _t5_bucket(rel):
    n = jnp.maximum(rel, 0)
    exact = N_BUCKETS // 2
    nf = jnp.maximum(n, exact).astype(F32)
    big = exact + (jnp.log(nf / exact) / math.log(MAX_DIST / exact) * (N_BUCKETS - exact)).astype(I32)
    return jnp.where(n < exact, n, jnp.minimum(big, N_BUCKETS - 1))


def _sort_key(s):
    bits = pltpu.bitcast(s, I32)
    return jnp.where(bits < 0, bits ^ jnp.int32(0x7FFFFFFF), bits)


def _lane(shape):
    return lax.broadcasted_iota(I32, shape, 1)


def _row(shape):
    return lax.broadcasted_iota(I32, shape, 0)


def _in_kernel(x_ref, g_ref, w_ref, o_ref):
    h = _rms(x_ref[...], g_ref[...])
    o_ref[...] = _dot(h, w_ref[...])


def _proj_in(x, g, w, tm):
    n, d = x.shape
    return pl.pallas_call(
        _in_kernel,
        grid=(n // tm,),
        in_specs=[pl.BlockSpec((tm, d), lambda i: (i, 0)),
                  pl.BlockSpec((1, d), lambda i: (0, 0)),
                  pl.BlockSpec((d, P_COLS), lambda i: (0, 0))],
        out_specs=pl.BlockSpec((tm, P_COLS), lambda i: (i, 0)),
        out_shape=jax.ShapeDtypeStruct((n, P_COLS), F32),
        compiler_params=_cparams("parallel"),
        name="proj_in",
    )(x, g, w)


def _rope_lanes(x, cos, sin_s):
    lane = _lane(x.shape)
    sw = jnp.where((lane % D_ROPE) < D_ROPE // 2, pltpu.roll(x, LANES - D_ROPE // 2, 1),
                   pltpu.roll(x, D_ROPE // 2, 1))
    return x * cos + sw * sin_s


def _mla_prep_kernel(cq_ref, ckv_ref, ckr_ref, cos_ref, sin_ref, cqg_ref, ckvg_ref, wuq_ref, wuk_ref,
                     rc_ref, ql_ref, qr_ref):
    cq = cq_ref[...]
    ms = jnp.sum(cq * cq, axis=-1, keepdims=True) * (1.0 / Q_LORA)
    hq = cq * lax.rsqrt(ms + EPS) * cqg_ref[...]
    qc = _dot(hq, wuq_ref[...])
    cos = cos_ref[...]
    sin_s = sin_ref[...]
    qr_ref[...] = _rope_lanes(qc[:, H_C * D_NOPE:], cos, sin_s)
    ql_ref[...] = _dot(qc[:, :H_C * D_NOPE], wuk_ref[...])
    rc_ref[:, 0:KV_LORA] = _rms(ckv_ref[...], ckvg_ref[...])
    rc_ref[:, KV_LORA:ROW_C] = _rope_lanes(ckr_ref[...], cos, sin_s)[:, 0:D_ROPE]


def _mla_prep(p, cos_t, sin_t, cqg, ckvg, wuq, wukbd, tm):
    n = p.shape[0]
    blk = lambda w, c: pl.BlockSpec((tm, w), lambda i, c=c: (i, c))
    cst = lambda a: pl.BlockSpec(a.shape, lambda i: (0,) * a.ndim)
    return pl.pallas_call(
        _mla_prep_kernel,
        grid=(n // tm,),
        in_specs=[blk(256, COL_CQ // 256), blk(128, COL_CKV // 128), blk(128, COL_CKR // 128),
                  blk(128, 0), blk(128, 0), cst(cqg), cst(ckvg), cst(wuq), cst(wukbd)],
        out_specs=[pl.BlockSpec((tm, ROW_C), lambda i: (i, 0)),
                   pl.BlockSpec((tm, H_C * KV_LORA), lambda i: (i, 0)),
                   pl.BlockSpec((tm, LANES), lambda i: (i, 0))],
        out_shape=[jax.ShapeDtypeStruct((n, ROW_C), F32),
                   jax.ShapeDtypeStruct((n, H_C * KV_LORA), F32),
                   jax.ShapeDtypeStruct((n, LANES), F32)],
        compiler_params=_cparams("parallel"),
        name="mla_prep",
    )(p, p, p, cos_t, sin_t, cqg, ckvg, wuq, wukbd)


def _merge_kernel(x_ref, oa_ref, ob_ref, oc_ref, od_ref, g_ref, wg_ref, wb_ref, wo_ref, o_ref):
    x = x_ref[...]
    h = _rms(x, g_ref[...]).astype(BF16)
    d = x.shape[1]
    mix = None
    for m, o_m in enumerate((oa_ref, ob_ref, oc_ref, od_ref)):
        gate = jax.nn.sigmoid(jnp.dot(h, wg_ref[:, m * d:(m + 1) * d], preferred_element_type=F32))
        term = gate * _dot(o_m[...], wb_ref[m])
        mix = term if mix is None else mix + term
    o_ref[...] = x + _dot(mix, wo_ref[...])


def _merge(x, oa, ob, oc, od, g, wg, wb, wo, tm):
    n, d = x.shape
    row = lambda w: pl.BlockSpec((tm, w), lambda i: (i, 0))
    cst = lambda a: pl.BlockSpec(a.shape, lambda i: (0,) * a.ndim)
    return pl.pallas_call(
        _merge_kernel,
        grid=(n // tm,),
        in_specs=[row(d), row(BRANCH_W), row(BRANCH_W), row(BRANCH_W), row(BRANCH_W),
                  cst(g), cst(wg), cst(wb), cst(wo)],
        out_specs=row(d),
        out_shape=jax.ShapeDtypeStruct((n, d), F32),
        compiler_params=_cparams("parallel"),
        name="merge",
    )(x, oa, ob, oc, od, g, wg, wb, wo)


def _ple_kernel(x_ref, po_ref, p_ref, pg_ref, pp_ref, gf_ref, o_ref, y_ref):
    x2 = x_ref[...] + po_ref[...]
    xn = x2 + jax.nn.sigmoid(_dot(x2, pg_ref[...])) * _dot(p_ref[...], pp_ref[...])
    o_ref[...] = xn
    y_ref[...] = _rms(xn, gf_ref[...])


def _ple(x, po, p, pg, pp, gf, tm):
    n, d = x.shape
    row = lambda w: pl.BlockSpec((tm, w), lambda i: (i, 0))
    cst = lambda a: pl.BlockSpec(a.shape, lambda i: (0,) * a.ndim)
    return pl.pallas_call(
        _ple_kernel,
        grid=(n // tm,),
        in_specs=[row(d), row(d), row(p.shape[1]), cst(pg), cst(pp), cst(gf)],
        out_specs=[row(d), row(d)],
        out_shape=[jax.ShapeDtypeStruct((n, d), F32), jax.ShapeDtypeStruct((n, d), F32)],
        compiler_params=_cparams("parallel"),
        name="ple",
    )(x, po, p, pg, pp, gf)


def _top_values(w, k, out_ref, idx):
    def body(r, w):
        m = jnp.max(w, axis=0, keepdims=True)
        out_ref[idx, pl.ds(r, 1), :] = m
        return jnp.where(w == m, -jnp.inf, w)
    lax.fori_loop(0, k, body, w)


def _peer_sel_kernel(x_ref, g_ref, wq_ref, sk_ref, s0_ref, s1_ref, e0_ref, e1_ref, tau_ref, sv_s, top_s):
    hn = _rms(x_ref[...], g_ref[...])
    q = _dot(hn, wq_ref[...])
    for hp in range(2 * PEER_HEADS):
        s = _dot_nt(sk_ref[hp], q[:, hp * N_KEYS:(hp + 1) * N_KEYS])
        if hp % 2 == 0:
            s0_ref[hp // 2] = s
        else:
            s1_ref[hp // 2] = s
        _top_values(s, PEER_TOPK, sv_s, hp)
    for h in range(PEER_HEADS):
        sv0 = sv_s[2 * h]
        sv1 = sv_s[2 * h + 1]
        cand = jnp.concatenate([sv0[a:a + 1, :] + sv1 for a in range(PEER_TOPK)], axis=0)
        _top_values(cand, PEER_TOPK, top_s, 0)
        top = top_s[0]
        z = jnp.sum(jnp.exp(top - top[0:1, :]), axis=0, keepdims=True)
        tau_ref[pl.ds(h, 1), :] = top[PEER_TOPK - 1:PEER_TOPK, :]
        e0_ref[h] = jnp.exp(s0_ref[h] - sv0[0:1, :]) / z
        e1_ref[h] = jnp.exp(s1_ref[h] - sv1[0:1, :])


def _peer_select(x, g, wq, sk, tn):
    n, d = x.shape
    hk = jax.ShapeDtypeStruct((PEER_HEADS, N_KEYS, n), F32)
    hk_spec = pl.BlockSpec((PEER_HEADS, N_KEYS, tn), lambda i: (0, 0, i))
    cst = lambda a: pl.BlockSpec(a.shape, lambda i: (0,) * a.ndim)
    return pl.pallas_call(
        _peer_sel_kernel,
        grid=(n // tn,),
        in_specs=[pl.BlockSpec((tn, d), lambda i: (i, 0)), cst(g), cst(wq), cst(sk)],
        out_specs=[hk_spec, hk_spec, hk_spec, hk_spec, pl.BlockSpec((PEER_HEADS, tn), lambda i: (0, i))],
        out_shape=[hk, hk, hk, hk, jax.ShapeDtypeStruct((PEER_HEADS, n), F32)],
        scratch_shapes=[pltpu.VMEM((2 * PEER_HEADS, PEER_TOPK, tn), F32),
                        pltpu.VMEM((1, PEER_TOPK, tn), F32)],
        compiler_params=_cparams("parallel"),
        name="peer_select",
    )(x, g, wq, sk)


def _peer_dense_kernel(x_ref, g_ref, u_ref, vt_ref, s0_ref, s1_ref, e0_ref, e1_ref, tau_ref, o_ref,
                       hn_s, acc_s, z_s, *, ic):
    c = pl.program_id(1)

    @pl.when(c == 0)
    def _():
        hn_s[...] = _rms(x_ref[...], g_ref[...]).astype(BF16)
        acc_s[...] = jnp.zeros_like(acc_s)

    act = lax.dot_general(u_ref[...], hn_s[...], (((1,), (1,)), ((), ())), preferred_element_type=F32)
    act = jax.nn.gelu(act)
    tau = tau_ref[...]
    for ii in range(ic):
        wt = None
        for h in range(PEER_HEADS):
            cnd = s0_ref[h, ii:ii + 1, :] + s1_ref[h]
            w = e0_ref[h, ii:ii + 1, :] * e1_ref[h]
            t = jnp.where(cnd >= tau[h:h + 1, :], w, 0.0)
            wt = t if wt is None else wt + t
        z_s[ii * N_KEYS:(ii + 1) * N_KEYS, :] = (wt * act[ii * N_KEYS:(ii + 1) * N_KEYS, :]).astype(BF16)
    acc_s[...] += jnp.dot(vt_ref[...], z_s[...], preferred_element_type=F32)

    @pl.when(c == pl.num_programs(1) - 1)
    def _():
        o_ref[...] = acc_s[...].T


def _peer_dense(x, g, u, vt, s0, s1, e0, e1, tau, tn, ic):
    n, d = x.shape
    ne = u.shape[0]
    ec = ic * N_KEYS
    return pl.pallas_call(
        functools.partial(_peer_dense_kernel, ic=ic),
        grid=(n // tn, ne // ec),
        in_specs=[pl.BlockSpec((tn, d), lambda i, c: (i, 0)),
                  pl.BlockSpec((1, d), lambda i, c: (0, 0)),
                  pl.BlockSpec((ec, d), lambda i, c: (c, 0)),
                  pl.BlockSpec((d, ec), lambda i, c: (0, c)),
                  pl.BlockSpec((PEER_HEADS, ic, tn), lambda i, c: (0, c, i)),
                  pl.BlockSpec((PEER_HEADS, N_KEYS, tn), lambda i, c: (0, 0, i)),
                  pl.BlockSpec((PEER_HEADS, ic, tn), lambda i, c: (0, c, i)),
                  pl.BlockSpec((PEER_HEADS, N_KEYS, tn), lambda i, c: (0, 0, i)),
                  pl.BlockSpec((PEER_HEADS, tn), lambda i, c: (0, i))],
        out_specs=pl.BlockSpec((tn, d), lambda i, c: (i, 0)),
        out_shape=jax.ShapeDtypeStruct((n, d), F32),
        scratch_shapes=[pltpu.VMEM((tn, d), BF16), pltpu.VMEM((d, tn), F32), pltpu.VMEM((ec, tn), BF16)],
        compiler_params=_cparams("parallel", "arbitrary"),
        name="peer_dense",
    )(x, g, u, vt, s0, s1, e0, e1, tau)


def _hgrn_kernel(p_ref, lb_ref, dg_ref, ones_ref, st0_ref, o_ref, st_ref, q_s, k_s, b_s, v_s, o_s,
                 *, ts, ch):
    s = pl.program_id(1)
    w = H_D * DK_D

    @pl.when(s == 0)
    def _():
        st_ref[...] = st0_ref[...]

    dq = p_ref[:, 0:w]
    df = p_ref[:, w:2 * w]
    lb = lb_ref[...]
    sig = jax.nn.sigmoid(df)
    logf = jnp.log(jnp.maximum(lb + (1.0 - lb) * sig, F_MIN))
    r = _row((ts, w)) % ch
    b = logf
    sh = 1
    while sh < ch:
        b = b + jnp.where(r >= sh, pltpu.roll(b, sh, 0), 0.0)
        sh *= 2
    q_s[...] = dq * jax.nn.sigmoid(dq)
    k_s[...] = (1.0 - lb) * (1.0 - sig)
    b_s[...] = b
    v_s[...] = p_ref[:, 2 * w:3 * w]
    ones_bd = ones_ref[...]
    bd_mask = ones_bd.astype(F32)
    srow = _row((ch, w))

    def body(c, carry):
        off = pl.multiple_of(c * ch, ch)
        q = q_s[pl.ds(off, ch), :]
        kk = k_s[pl.ds(off, ch), :]
        bb = b_s[pl.ds(off, ch), :]
        v = v_s[pl.ds(off, ch), :]
        st = st_ref[0]
        o_inter = _dot_nt(q * jnp.exp(bb), st)
        rows = []
        for t in range(ch):
            dec = jnp.exp(jnp.where(srow <= t, bb[t:t + 1, :] - bb, 0.0))
            rows.append(jnp.where(srow <= t, dec * kk * q[t:t + 1, :], 0.0))
        pm = jnp.concatenate(rows, axis=0)
        rr = jnp.dot(pm.astype(BF16), ones_bd, preferred_element_type=F32)
        o_intra = jnp.sum(rr.reshape(ch, ch, w) * v[None, :, :], axis=1)
        o_s[pl.ds(off, ch), :] = o_inter + o_intra
        bl = bb[ch - 1:ch, :]
        kdec = kk * jnp.exp(bl - bb)
        upd = lax.dot_general(v.astype(BF16), kdec.astype(BF16), (((0,), (0,)), ((), ())),
                              preferred_element_type=F32)
        st_ref[0] = (st * jnp.exp(bl) + upd) * bd_mask
        return carry

    lax.fori_loop(0, ts // ch, body, 0)
    o = o_s[...]
    ms = jnp.dot((o * o).astype(BF16), ones_bd, preferred_element_type=F32) * (1.0 / DV_D)
    dg = p_ref[:, 3 * w:4 * w]
    o_ref[...] = o * lax.rsqrt(ms + EPS) * dg_ref[...] * (dg * jax.nn.sigmoid(dg))


def _hgrn(p, lb, dg, ones_bd, st0, nb, t, row0, ts, ch):
    w = H_D * DK_D
    ns = t // ts
    rb0 = row0 // ts
    cst = lambda a: pl.BlockSpec(a.shape, lambda b, s: (0,) * a.ndim)
    return pl.pallas_call(
        functools.partial(_hgrn_kernel, ts=ts, ch=ch),
        grid=(nb, ns),
        in_specs=[pl.BlockSpec((ts, 4 * w), lambda b, s: (rb0 + b * ns + s, COL_D // (4 * w))),
                  cst(lb), cst(dg), cst(ones_bd),
                  pl.BlockSpec((1, w, w), lambda b, s: (b, 0, 0))],
        out_specs=[pl.BlockSpec((ts, w), lambda b, s: (b * ns + s, 0)),
                   pl.BlockSpec((1, w, w), lambda b, s: (b, 0, 0))],
        out_shape=[jax.ShapeDtypeStruct((nb * t, w), F32), jax.ShapeDtypeStruct((nb, w, w), F32)],
        scratch_shapes=[pltpu.VMEM((ts, w), F32)] * 5,
        compiler_params=_cparams("parallel", "arbitrary"),
        name="hgrn",
    )(p, lb, dg, ones_bd, st0)


def _radix_threshold(count_ge, k, rows):
    def body(i, t):
        cand = t ^ jnp.left_shift(jnp.int32(1), 31 - i)
        return jnp.where(count_ge(cand) >= k, cand, t)
    return lax.fori_loop(0, 32, body, jnp.full((rows, 1), INT_MIN, I32))


def _head_rows_low(q, scale):
    lane = _lane((q.shape[0], LANES))
    out = []
    for m in range(2):
        tile = q[:, m * LANES:(m + 1) * LANES] * scale
        out.append(jnp.where(lane < DH, tile, 0.0))
        out.append(jnp.where(lane < DH, pltpu.roll(tile, DH, 1), 0.0))
    return jnp.concatenate(out, axis=0)


def _head_rows_group(q, scale):
    lane = _lane((q.shape[0], LANES))
    t0 = q[:, 0:LANES] * scale
    t1 = q[:, LANES:2 * LANES] * scale
    return jnp.concatenate([
        jnp.where(lane < DH, t0, 0.0),
        jnp.where(lane < DH, pltpu.roll(t0, DH, 1), 0.0),
        jnp.where(lane >= DH, pltpu.roll(t1, DH, 1), 0.0),
        jnp.where(lane >= DH, t1, 0.0)], axis=0)


def _idx_heads(iq):
    lane = _lane(iq.shape)
    return [jnp.where(lane < D_IDX, iq if h == 0 else pltpu.roll(iq, LANES - D_IDX * h, 1), 0.0)
            for h in range(H_IDX)]


def _index_score(iqh, iw, ik_tile):
    s = None
    for h in range(H_IDX):
        r = jnp.maximum(_dot3_nt(iqh[h], ik_tile), 0.0)
        term = iw[:, h:h + 1] * r
        s = term if s is None else s + term
    return s * ((H_IDX * D_IDX) ** -0.5) + 0.0


def _softmax_step(lg, vals, m_i, l_i, acc):
    m_new = jnp.maximum(m_i, jnp.max(lg, axis=1, keepdims=True))
    alpha = jnp.exp(m_i - m_new)
    p = jnp.exp(lg - m_new)
    l_new = alpha * l_i + jnp.sum(p, axis=1, keepdims=True)
    return m_new, l_new, alpha * acc + _dot(p, vals)


def _dsa_p_kernel(q_ref, qkv_ref, iq_ref, kv_ref, bt_ref, tri_ref, o_ref, key_s, *, tq, chk, topk):
    qi = pl.program_id(1)
    nch = (qi * tq + tq + chk - 1) // chk
    qpos = qi * tq + _row((tq, 1))
    iw = qkv_ref[:, ROW_A:ROW_A + H_IDX]
    iqh = _idx_heads(iq_ref[...])
    nv = bt_ref.shape[0]

    def kpos_of(off):
        return off + _lane((tq, chk))

    def score_body(c, carry):
        off = pl.multiple_of(c * chk, chk)
        s = _index_score(iqh, iw, kv_ref[pl.ds(off, chk), LANES:2 * LANES])
        s = jnp.where(kpos_of(off) <= qpos, s, NEG)
        key_s[:, pl.ds(off, chk)] = _sort_key(s)
        return carry

    lax.fori_loop(0, nch, score_body, 0)

    def count(pred):
        def body(c, acc):
            off = pl.multiple_of(c * chk, chk)
            m = pred(key_s[:, pl.ds(off, chk)]).astype(I32)
            for l in range(chk // LANES):
                acc = acc + m[:, l * LANES:(l + 1) * LANES]
            return acc
        acc = lax.fori_loop(0, nch, body, jnp.zeros((tq, LANES), I32))
        return jnp.sum(acc, axis=1, keepdims=True)

    thr = _radix_threshold(lambda cand: count(lambda kc: kc >= cand), topk, tq)
    need = (topk - count(lambda kc: kc > thr)).astype(F32)

    qs = _head_rows_low(q_ref[...], DH ** -0.5).astype(BF16)

    def att_body(c, carry):
        m_i, l_i, acc, run = carry
        off = pl.multiple_of(c * chk, chk)
        kc = key_s[:, pl.ds(off, chk)]
        eq = kc == thr
        eqf = jnp.where(eq, 1.0, 0.0)
        pref = jnp.dot(eqf.astype(BF16), tri_ref[...], preferred_element_type=F32) + run
        sel = ((kc > thr) | (eq & (pref < need))) & (kpos_of(off) <= qpos)
        kvt = kv_ref[pl.ds(off, chk), 0:LANES].astype(BF16)
        lg = lax.dot_general(qs, kvt, (((1,), (1,)), ((), ())), preferred_element_type=F32)
        r = jnp.minimum(qi * (tq // LANES) - c * (chk // LANES), nv - 1)
        lg = lg + bt_ref[r]
        lg = lg + jnp.concatenate([jnp.where(sel, 0.0, NEG)] * H_A, axis=0)
        m_i, l_i, acc = _softmax_step(lg, kvt, m_i, l_i, acc)
        return m_i, l_i, acc, run + jnp.sum(eqf, axis=1, keepdims=True)

    init = (jnp.full((H_A * tq, 1), -jnp.inf, F32), jnp.zeros((H_A * tq, 1), F32),
            jnp.zeros((H_A * tq, LANES), F32), jnp.zeros((tq, 1), F32))
    _, l_i, acc, _ = lax.fori_loop(0, nch, att_body, init)
    o = acc / l_i
    lane = _lane((tq, LANES))
    for m in range(2):
        o_ref[:, m * LANES:(m + 1) * LANES] = jnp.where(
            lane < DH, pltpu.roll(o[2 * m * tq:(2 * m + 1) * tq], DH, 1), o[(2 * m + 1) * tq:(2 * m + 2) * tq])


def _dsa_prompt(p, bt, tri, nb, t, tq, chk, topk):
    nq = t // tq
    return pl.pallas_call(
        functools.partial(_dsa_p_kernel, tq=tq, chk=chk, topk=topk),
        grid=(nb, nq),
        in_specs=[pl.BlockSpec((tq, 256), lambda b, i: (b * nq + i, COL_AQ // 256)),
                  pl.BlockSpec((tq, 256), lambda b, i: (b * nq + i, COL_AKV // 256)),
                  pl.BlockSpec((tq, 128), lambda b, i: (b * nq + i, COL_AIQ // 128)),
                  pl.BlockSpec((t, 256), lambda b, i: (b, COL_AKV // 256)),
                  pl.BlockSpec(bt.shape, lambda b, i: (0, 0, 0)),
                  pl.BlockSpec(tri.shape, lambda b, i: (0, 0))],
        out_specs=pl.BlockSpec((tq, BRANCH_W), lambda b, i: (b * nq + i, 0)),
        out_shape=jax.ShapeDtypeStruct((nb * t, BRANCH_W), F32),
        scratch_shapes=[pltpu.VMEM((tq, t), I32)],
        compiler_params=_cparams("parallel", "arbitrary"),
        name="dsa_prompt",
    )(p, p, p, p, bt, tri)


def _moba_p_kernel(q_ref, kv_ref, bt_ref, o_ref, m_s, l_s, acc_s, *, nblk, ntop):
    qi = pl.program_id(1)
    tq = MOBA_BLOCK
    rows = H_B * tq
    qs = _head_rows_group(q_ref[...], DH ** -0.5)
    kmean = jnp.sum(kv_ref[:, 0:LANES].reshape(nblk, MOBA_BLOCK, LANES), axis=1) * (1.0 / MOBA_BLOCK)
    gate = _dot3_nt(qs, kmean)
    n_l = _lane((rows, nblk))
    gate = jnp.where(n_l < qi, gate, NEG)
    rank = jnp.zeros((rows, nblk), I32)
    for m in range(nblk):
        col = gate[:, m:m + 1]
        beats = (col > gate) | ((col == gate) & (m < n_l))
        rank = rank + beats.astype(I32)
    bm = ((rank < ntop) & (gate > 0.5 * NEG)) | (n_l == qi)
    bmf = jnp.where(bm, 1.0, 0.0)
    m_s[...] = jnp.full(m_s.shape, -jnp.inf, F32)
    l_s[...] = jnp.zeros(l_s.shape, F32)
    acc_s[...] = jnp.zeros(acc_s.shape, F32)
    qb = qs.astype(BF16)
    qpos = qi * tq + _row((tq, 1))
    qpos4 = jnp.concatenate([qpos] * H_B, axis=0)
    for n in range(nblk):
        @pl.when(n <= qi)
        def _(n=n):
            kt = kv_ref[n * MOBA_BLOCK:(n + 1) * MOBA_BLOCK, 0:LANES].astype(BF16)
            vt = kv_ref[n * MOBA_BLOCK:(n + 1) * MOBA_BLOCK, LANES:2 * LANES].astype(BF16)
            lg = lax.dot_general(qb, kt, (((1,), (1,)), ((), ())), preferred_element_type=F32)
            lg = lg + bt_ref[jnp.minimum(qi - n, bt_ref.shape[0] - 1)]
            kpos = n * MOBA_BLOCK + _lane((rows, MOBA_BLOCK))
            ok = (bmf[:, n:n + 1] > 0.5) & (kpos <= qpos4)
            lg = jnp.where(ok, lg, NEG)
            m_i, l_i, acc = _softmax_step(lg, vt, m_s[...], l_s[...], acc_s[...])
            m_s[...] = m_i
            l_s[...] = l_i
            acc_s[...] = acc
    o = acc_s[...] / l_s[...]
    lane = _lane((tq, LANES))
    o_ref[:, 0:LANES] = jnp.where(lane < DH, o[0:tq], pltpu.roll(o[tq:2 * tq], DH, 1))
    o_ref[:, LANES:2 * LANES] = jnp.where(lane < DH, pltpu.roll(o[2 * tq:3 * tq], DH, 1), o[3 * tq:4 * tq])


def _moba_prompt(p, bt, nb, t):
    tq = MOBA_BLOCK
    nq = t // tq
    rows = H_B * tq
    return pl.pallas_call(
        functools.partial(_moba_p_kernel, nblk=nq, ntop=min(MOBA_TOPK, nq)),
        grid=(nb, nq),
        in_specs=[pl.BlockSpec((tq, 256), lambda b, i: (b * nq + i, COL_BQ // 256)),
                  pl.BlockSpec((t, 256), lambda b, i: (b, COL_BKV // 256)),
                  pl.BlockSpec(bt.shape, lambda b, i: (0, 0, 0))],
        out_specs=pl.BlockSpec((tq, BRANCH_W), lambda b, i: (b * nq + i, 0)),
        out_shape=jax.ShapeDtypeStruct((nb * t, BRANCH_W), F32),
        scratch_shapes=[pltpu.VMEM((rows, 1), F32), pltpu.VMEM((rows, 1), F32), pltpu.VMEM((rows, LANES), F32)],
        compiler_params=_cparams("parallel", "arbitrary"),
        name="moba_prompt",
    )(p, p, bt)


def _mla_q_rows(ql, qr):
    qlat = jnp.concatenate([ql[:, h * KV_LORA:(h + 1) * KV_LORA] for h in range(H_C)], axis=0)
    qrope = jnp.concatenate([qr[:, h * D_ROPE:(h + 1) * D_ROPE] for h in range(H_C)], axis=0)
    return qlat.astype(BF16), qrope.astype(BF16)


def _mla_out(o_lat, wuv_ref, t):
    out = None
    for h in range(H_C):
        term = _dot(o_lat[h * t:(h + 1) * t], wuv_ref[h])
        out = term if out is None else out + term
    return out


def _mla_p_kernel(ql_ref, qr_ref, rc_ref, wuv_ref, o_ref, *, tq, chk):
    qi = pl.program_id(1)
    nch = (qi * tq + tq + chk - 1) // chk
    rows = H_C * tq
    qlat, qrope = _mla_q_rows(ql_ref[...], qr_ref[...])
    qpos = qi * tq + _row((tq, 1))
    qpos4 = jnp.concatenate([qpos] * H_C, axis=0)
    scale = (D_NOPE + D_ROPE) ** -0.5
    dn = (((1,), (1,)), ((), ()))

    def body(c, carry):
        m_i, l_i, acc = carry
        off = pl.multiple_of(c * chk, chk)
        ckv = rc_ref[pl.ds(off, chk), 0:KV_LORA].astype(BF16)
        kr = rc_ref[pl.ds(off, chk), KV_LORA:ROW_C].astype(BF16)
        s = (lax.dot_general(qlat, ckv, dn, preferred_element_type=F32)
             + lax.dot_general(qrope, kr, dn, preferred_element_type=F32)) * scale
        kpos = off + _lane((rows, chk))
        s = jnp.where(kpos <= qpos4, s, NEG)
        return _softmax_step(s, ckv, m_i, l_i, acc)

    init = (jnp.full((rows, 1), -jnp.inf, F32), jnp.zeros((rows, 1), F32), jnp.zeros((rows, KV_LORA), F32))
    _, l_i, acc = lax.fori_loop(0, nch, body, init)
    o_ref[...] = _mla_out(acc / l_i, wuv_ref, tq)


def _mla_prompt(ql, qr, rc, wuvp, nb, t, tq, chk):
    nq = t // tq
    return pl.pallas_call(
        functools.partial(_mla_p_kernel, tq=tq, chk=chk),
        grid=(nb, nq),
        in_specs=[pl.BlockSpec((tq, H_C * KV_LORA), lambda b, i: (b * nq + i, 0)),
                  pl.BlockSpec((tq, LANES), lambda b, i: (b * nq + i, 0)),
                  pl.BlockSpec((t, ROW_C), lambda b, i: (b, 0)),
                  pl.BlockSpec(wuvp.shape, lambda b, i: (0, 0, 0))],
        out_specs=pl.BlockSpec((tq, BRANCH_W), lambda b, i: (b * nq + i, 0)),
        out_shape=jax.ShapeDtypeStruct((nb * t, BRANCH_W), F32),
        compiler_params=_cparams("parallel", "arbitrary"),
        name="mla_prompt",
    )(ql, qr, rc, wuvp)


def _page_specs(shape_tail, li, pp):
    nd = len(shape_tail)
    return [pl.BlockSpec((None, None) + shape_tail,
                         lambda b, s, pt, j=j: (li, pt[b, s * pp + j]) + (0,) * nd)
            for j in range(pp)]


def _pad_rows(x, rows):
    return jnp.concatenate([x, jnp.zeros((rows - x.shape[0], x.shape[1]), x.dtype)], axis=0)


def _masked_attend(lg, neg_mask, vals):
    lg = lg + neg_mask
    m = jnp.max(lg, axis=1, keepdims=True)
    p = jnp.exp(lg - m)
    return _dot(p, vals) / jnp.sum(p, axis=1, keepdims=True)


def _dsa_s_kernel(pt_ref, *refs, pp, past, topk, tdec):
    pages = refs[:pp]
    q_ref, qkv_ref, iq_ref, cb_ref, tail_ref, tri_ref, o_ref, key_s, lg_s, kv_s = refs[pp:]
    s = pl.program_id(1)
    lp = past + LANES
    iw = qkv_ref[:, ROW_A:ROW_A + H_IDX]
    iqh = _idx_heads(iq_ref[...])
    iq32 = [x[:, 0:D_IDX] for x in iqh]
    qs = _head_rows_low(q_ref[...], DH ** -0.5).astype(BF16)
    dn = (((1,), (1,)), ((), ()))
    for j in range(pp):
        off = pl.multiple_of((s * pp + j) * LANES, LANES)
        kvt = pages[j][:, 0:LANES].astype(BF16)
        sc = _index_score(iq32, iw, pages[j][:, LANES:ROW_A])
        key_s[:, pl.ds(off, LANES)] = _sort_key(sc)
        lg_s[:, pl.ds(off, LANES)] = lax.dot_general(qs, kvt, dn, preferred_element_type=F32)
        kv_s[pl.ds(off, LANES), :] = kvt

    @pl.when(s == pl.num_programs(1) - 1)
    def _():
        newt = _pad_rows(qkv_ref[:, 0:LANES], LANES).astype(BF16)
        kv_s[past:lp, :] = newt
        sc = _index_score(iqh, iw, _pad_rows(qkv_ref[:, LANES:2 * LANES], LANES))
        causal = _lane((tdec, LANES)) <= _row((tdec, LANES))
        key_s[:, past:lp] = _sort_key(jnp.where(causal, sc, NEG))
        lg_s[:, past:lp] = lax.dot_general(qs, newt, dn, preferred_element_type=F32)
        keys = key_s[...]
        thr = _radix_threshold(
            lambda cand: jnp.sum((keys >= cand).astype(I32), axis=1, keepdims=True), topk, tdec)
        need = (topk - jnp.sum((keys > thr).astype(I32), axis=1, keepdims=True)).astype(F32)
        eq = keys == thr
        eqf = jnp.where(eq, 1.0, 0.0)
        nck = lp // LANES
        stacked = jnp.concatenate([eqf[:, c * LANES:(c + 1) * LANES] for c in range(nck)],
                                  axis=0).astype(BF16)
        pin = jnp.dot(stacked, tri_ref[...], preferred_element_type=F32)
        tot = jnp.dot(stacked, jnp.ones((LANES, LANES), BF16), preferred_element_type=F32)
        run = jnp.zeros((tdec, LANES), F32)
        pref = []
        for c in range(nck):
            pref.append(pin[c * tdec:(c + 1) * tdec] + run)
            run = run + tot[c * tdec:(c + 1) * tdec]
        pref = jnp.concatenate(pref, axis=1)
        kpos = _lane((tdec, lp))
        sel = ((keys > thr) | (eq & (pref < need))) & ((kpos < past) | (kpos - past <= _row((tdec, lp))))
        lg = lg_s[...] + cb_ref[...]
        lg = jnp.concatenate([lg[:, :lp - 2 * LANES], lg[:, lp - 2 * LANES:] + tail_ref[...]], axis=1)
        o = _masked_attend(lg, jnp.concatenate([jnp.where(sel, 0.0, NEG)] * H_A, axis=0), kv_s[...])
        lane = _lane((tdec, LANES))
        for m in range(2):
            o_ref[:, m * LANES:(m + 1) * LANES] = jnp.where(
                lane < DH, pltpu.roll(o[2 * m * tdec:(2 * m + 1) * tdec], DH, 1),
                o[(2 * m + 1) * tdec:(2 * m + 2) * tdec])


def _sample_call(kernel, name, cache, li, pt, pp, row_inputs, const_inputs, nb, tdec, row0, scratch):
    npages = pt.shape[1]
    rb0 = row0 // tdec
    in_specs = _page_specs(cache.shape[2:], li, pp)
    args = [cache] * pp
    for arr, width, col in row_inputs:
        in_specs.append(pl.BlockSpec((tdec, width), lambda b, s, pt, col=col: (rb0 + b, col)))
        args.append(arr)
    for arr in const_inputs:
        in_specs.append(pl.BlockSpec(arr.shape, lambda b, s, pt, nd=arr.ndim: (0,) * nd))
        args.append(arr)
    return pl.pallas_call(
        kernel,
        grid_spec=pltpu.PrefetchScalarGridSpec(
            num_scalar_prefetch=1,
            grid=(nb, npages // pp),
            in_specs=in_specs,
            out_specs=pl.BlockSpec((tdec, BRANCH_W), lambda b, s, pt: (b, 0)),
            scratch_shapes=scratch),
        out_shape=jax.ShapeDtypeStruct((nb * tdec, BRANCH_W), F32),
        compiler_params=_cparams("parallel", "arbitrary"),
        name=name,
    )(pt, *args)


def _moba_s_kernel(pt_ref, *refs, pp, past, ntop, tdec):
    pages = refs[:pp]
    q_ref, kv_ref, cb_ref, tail_ref, o_ref, lg_s, v_s, ks_s = refs[pp:]
    s = pl.program_id(1)
    lp = past + LANES
    nbk = past // MOBA_BLOCK
    ppb = MOBA_BLOCK // LANES
    rows = H_B * tdec
    qs = _head_rows_group(q_ref[...], DH ** -0.5)
    qb = qs.astype(BF16)
    dn = (((1,), (1,)), ((), ()))

    @pl.when(s == 0)
    def _():
        ks_s[...] = jnp.zeros_like(ks_s)

    for j in range(pp):
        off = pl.multiple_of((s * pp + j) * LANES, LANES)
        kt = pages[j][:, 0:LANES]
        lg_s[:, pl.ds(off, LANES)] = lax.dot_general(qb, kt.astype(BF16), dn, preferred_element_type=F32)
        v_s[pl.ds(off, LANES), :] = pages[j][:, LANES:2 * LANES].astype(BF16)
        blk = s * (pp // ppb) + j // ppb
        ks_s[pl.ds(blk, 1), :] = ks_s[pl.ds(blk, 1), :] + jnp.sum(kt, axis=0, keepdims=True)

    @pl.when(s == pl.num_programs(1) - 1)
    def _():
        v_s[past:lp, :] = _pad_rows(kv_ref[:, LANES:2 * LANES], LANES).astype(BF16)
        knew = _pad_rows(kv_ref[:, 0:LANES], LANES).astype(BF16)
        lg_s[:, past:lp] = lax.dot_general(qb, knew, dn, preferred_element_type=F32)
        kmean = ks_s[...] * (1.0 / MOBA_BLOCK)
        gate = _dot3_nt(qs, kmean)
        n_l = _lane((rows, nbk))
        rank = jnp.zeros((rows, nbk), I32)
        for m in range(nbk):
            col = gate[:, m:m + 1]
            rank = rank + ((col > gate) | ((col == gate) & (m < n_l))).astype(I32)
        bm = (rank < ntop) & (gate > 0.5 * NEG)
        bneg = jnp.where(bm, 0.0, NEG)
        trow = jnp.concatenate([_row((tdec, LANES))] * H_B, axis=0)
        neg_mask = jnp.concatenate(
            [jnp.broadcast_to(bneg[:, n:n + 1], (rows, MOBA_BLOCK)) for n in range(nbk)]
            + [jnp.where(_lane((rows, LANES)) <= trow, 0.0, NEG)], axis=1)
        lg = lg_s[...] + cb_ref[...]
        lg = jnp.concatenate([lg[:, :lp - 2 * LANES], lg[:, lp - 2 * LANES:] + tail_ref[...]], axis=1)
        o = _masked_attend(lg, neg_mask, v_s[...])
        lane = _lane((tdec, LANES))
        o_ref[:, 0:LANES] = jnp.where(lane < DH, o[0:tdec], pltpu.roll(o[tdec:2 * tdec], DH, 1))
        o_ref[:, LANES:2 * LANES] = jnp.where(lane < DH, pltpu.roll(o[2 * tdec:3 * tdec], DH, 1),
                                              o[3 * tdec:4 * tdec])


def _mla_s_kernel(pt_ref, *refs, pp, tdec):
    pages = refs[:pp]
    ql_ref, qr_ref, rc_ref, wuv_ref, o_ref, m_s, l_s, acc_s = refs[pp:]
    s = pl.program_id(1)
    rows = H_C * tdec
    qlat, qrope = _mla_q_rows(ql_ref[...], qr_ref[...])
    scale = (D_NOPE + D_ROPE) ** -0.5
    dn = (((1,), (1,)), ((), ()))

    @pl.when(s == 0)
    def _():
        m_s[...] = jnp.full(m_s.shape, -jnp.inf, F32)
        l_s[...] = jnp.zeros_like(l_s)
        acc_s[...] = jnp.zeros_like(acc_s)

    def scores(ckv, kr):
        return (lax.dot_general(qlat, ckv, dn, preferred_element_type=F32)
                + lax.dot_general(qrope, kr, dn, preferred_element_type=F32)) * scale

    ckvs = [pages[j][:, 0:KV_LORA].astype(BF16) for j in range(pp)]
    sc = jnp.concatenate([scores(ckvs[j], pages[j][:, KV_LORA:ROW_C].astype(BF16)) for j in range(pp)], axis=1)
    m_i, l_i, acc = _softmax_step(sc, jnp.concatenate(ckvs, axis=0), m_s[...], l_s[...], acc_s[...])
    m_s[...] = m_i
    l_s[...] = l_i
    acc_s[...] = acc

    @pl.when(s == pl.num_programs(1) - 1)
    def _():
        ckv = _pad_rows(rc_ref[:, 0:KV_LORA], LANES).astype(BF16)
        kr = _pad_rows(rc_ref[:, KV_LORA:ROW_C], LANES).astype(BF16)
        trow = jnp.concatenate([_row((tdec, LANES))] * H_C, axis=0)
        sn = jnp.where(_lane((rows, LANES)) <= trow, scores(ckv, kr), NEG)
        _, l_f, acc_f = _softmax_step(sn, ckv, m_s[...], l_s[...], acc_s[...])
        o_ref[...] = _mla_out(acc_f / l_f, wuv_ref, tdec)


def _bias_tables(tab, n_var, tq, chk, step):
    heads = tab.shape[1]
    i = jnp.arange(tq, dtype=I32)[:, None]
    j = jnp.arange(chk, dtype=I32)[None, :]
    out = []
    for r in range(n_var):
        bk = _t5_bucket(i - j + step * r)
        out.append(jnp.moveaxis(tab[bk], -1, 0).reshape(heads * tq, chk))
    return jnp.stack(out).astype(F32)


def _tail_tables(tab, tdec):
    heads = tab.shape[1]
    t = jnp.arange(tdec, dtype=I32)[:, None]
    j = jnp.arange(2 * LANES, dtype=I32)[None, :]
    bk = _t5_bucket(t + LANES - j)
    full = jnp.moveaxis(tab[bk], -1, 0).reshape(heads * tdec, 2 * LANES).astype(F32)
    far = tab[_t5_bucket(jnp.int32(1 << 20))].astype(F32)
    cb = jnp.repeat(far, tdec)[:, None]
    return cb, full - cb


def _pick_tile(n, cap, mult):
    best = mult
    for c in range(mult, cap + 1, mult):
        if n % c == 0:
            best = c
    return best


def kernel(x_prompt, x_sample, cache_a, cache_b, cache_c, state_d, page_table, p_prompt, p_sample, norm1_g, w_in, cq_norm_g, w_uq, ckv_norm_g, w_uk, w_uv, lb_logits, d_norm_g, t5_bias, w_branch, w_out, norm2_g, peer_wq, peer_subkeys, peer_u, peer_v, ple_gate, ple_proj, final_norm_g):
    bp, tp, d = x_prompt.shape
    bs, ts, _ = x_sample.shape
    depth = w_in.shape[0]
    npages = page_table.shape[1]
    page = cache_a.shape[2]
    past = npages * page
    n_p = bp * tp
    n_s = bs * ts
    n = n_p + n_s
    assert page == LANES and ts == 8 and tp % MOBA_BLOCK == 0 and past % MOBA_BLOCK == 0
    tm = _pick_tile(math.gcd(n_p, n_s), 256, 8)
    assert n_p % tm == 0 and n % LANES == 0
    tn_peer = _pick_tile(n, 640, LANES)
    tq = 128
    chk = min(512, tp)
    pp = 8 if npages % 8 == 0 else 2
    assert tp % chk == 0 and npages % pp == 0

    x = jnp.concatenate([x_prompt.reshape(n_p, d), x_sample.reshape(n_s, d)], axis=0)
    p_all = jnp.concatenate([p_prompt.reshape(depth, n_p, -1), p_sample.reshape(depth, n_s, -1)], axis=1)

    sm = jax.nn.softmax(lb_logits.astype(F32), axis=0)
    lb_all = jnp.maximum(jnp.cumsum(sm, axis=0) - sm[0], 0.0)
    zc = lambda k: jnp.zeros((depth, d, k), F32)
    o = np.cumsum((0,) + (H_A * DH, DH, DH, H_IDX * D_IDX, H_IDX, D_IDX, H_B * DH, G_B * DH, G_B * DH,
                          Q_LORA, KV_LORA, D_ROPE, 4 * H_D * DK_D))
    seg = lambda a, b: w_in[:, :, o[a]:o[b]]
    w_cat = jnp.concatenate([
        seg(12, 13),
        seg(0, 1),
        seg(1, 3), seg(5, 6), seg(4, 5), zc(256 - ROW_A - H_IDX),
        seg(6, 7), seg(7, 9),
        seg(9, 10), zc(256 - Q_LORA),
        seg(3, 4), seg(10, 11), seg(11, 12), zc(LANES - D_ROPE)], axis=2).astype(BF16)
    w_gate = w_in[:, :, o[13]:].astype(BF16)
    cqg = jnp.pad(cq_norm_g, ((0, 0), (0, 256 - Q_LORA)))[:, None, :]
    wuq3 = w_uq.reshape(depth, Q_LORA, H_C, D_NOPE + D_ROPE)
    wuq_p = jnp.concatenate([wuq3[..., :D_NOPE].reshape(depth, Q_LORA, H_C * D_NOPE),
                             wuq3[..., D_NOPE:].reshape(depth, Q_LORA, H_C * D_ROPE)], axis=2)
    wuq_p = jnp.pad(wuq_p, ((0, 0), (0, 256 - Q_LORA), (0, 0))).astype(BF16)
    wukbd = jnp.zeros((depth, H_C, D_NOPE, H_C, KV_LORA), F32)
    wuvp = jnp.zeros((depth, H_C, KV_LORA, H_C, D_V_C), F32)
    for h in range(H_C):
        wukbd = wukbd.at[:, h, :, h, :].set(jnp.swapaxes(w_uk[:, :, h, :], 1, 2))
        wuvp = wuvp.at[:, h, :, h, :].set(w_uv[:, :, h, :])
    wukbd = wukbd.reshape(depth, H_C * D_NOPE, H_C * KV_LORA).astype(BF16)
    wuvp = wuvp.reshape(depth, H_C, KV_LORA, H_C * D_V_C).astype(BF16)
    dgt = jnp.tile(d_norm_g, (1, H_D))[:, None, :]
    hd = lax.broadcasted_iota(I32, (H_D * DK_D, H_D * DK_D), 0) // DK_D
    ones_bd = (hd == hd.T).astype(BF16)
    wb = w_branch.astype(BF16)
    wo = w_out.astype(BF16)
    wq = peer_wq.astype(BF16)
    sk = peer_subkeys.reshape(depth, 2 * PEER_HEADS, N_KEYS, -1).astype(BF16)
    u_b = peer_u.astype(BF16)
    vt_b = jnp.swapaxes(peer_v, 1, 2).astype(BF16)
    pg = ple_gate.astype(BF16)
    ppj = ple_proj.astype(BF16)

    pos = jnp.concatenate([jnp.tile(jnp.arange(tp, dtype=I32), bp),
                           jnp.tile(past + jnp.arange(ts, dtype=I32), bs)])
    half = D_ROPE // 2
    inv = 1.0 / (ROPE_THETA ** (jnp.arange(half, dtype=F32) / half))
    ang = pos.astype(F32)[:, None] * inv
    cos_t = jnp.tile(jnp.cos(ang), (1, LANES // half))
    sgn = jnp.where((jnp.arange(LANES) % D_ROPE) < half, -1.0, 1.0).astype(F32)
    sin_t = jnp.tile(jnp.sin(ang), (1, LANES // half)) * sgn
    bt_a = _bias_tables(t5_bias[:, :H_A], chk // LANES + 2, tq, chk, LANES)
    bt_b = _bias_tables(t5_bias[:, H_A:], 3, MOBA_BLOCK, MOBA_BLOCK, MOBA_BLOCK)
    cb_a, tail_a = _tail_tables(t5_bias[:, :H_A], ts)
    cb_b, tail_b = _tail_tables(t5_bias[:, H_A:], ts)
    tri_c = (lax.broadcasted_iota(I32, (chk, chk), 0) < lax.broadcasted_iota(I32, (chk, chk), 1)).astype(BF16)
    tri_l = tri_c[:LANES, :LANES]
    topk_p = min(TOPK_A_MAX, tp // 4)
    topk_s = min(TOPK_A_MAX, (past + ts) // 4)
    w = H_D * DK_D
    st0_p = jnp.zeros((bp, w, w), F32)
    lp = past + LANES
    rows_dec = H_A * ts

    rows_a, rows_b, rows_c, st_p, st_s = [], [], [], [], []
    y = None
    for li in range(depth):
        p = _proj_in(x, norm1_g[li][None, :], w_cat[li], tm)
        rc, ql, qr = _mla_prep(p, cos_t, sin_t, cqg[li], ckv_norm_g[li][None, :], wuq_p[li], wukbd[li], tm)
        rows_a.append(p[:, COL_AKV:COL_AKV + ROW_A])
        rows_b.append(p[:, COL_BKV:COL_BKV + 2 * G_B * DH])
        rows_c.append(rc)

        oa_p = _dsa_prompt(p, bt_a, tri_c, bp, tp, tq, chk, topk_p)
        ob_p = _moba_prompt(p, bt_b, bp, tp)
        oc_p = _mla_prompt(ql, qr, rc, wuvp[li], bp, tp, tq, chk)
        lbl = lb_all[li][None, :]
        od_p, stp = _hgrn(p, lbl, dgt[li], ones_bd, st0_p, bp, tp, 0, min(tp, 256), math.gcd(tp, CHUNK_D))
        st0_s = jnp.zeros((bs, H_D, DV_D, H_D, DK_D), F32)
        for h in range(H_D):
            st0_s = st0_s.at[:, h, :, h, :].set(jnp.swapaxes(state_d[li, :, h].astype(F32), 1, 2))
        st0_s = st0_s.reshape(bs, w, w)
        od_s, sts = _hgrn(p, lbl, dgt[li], ones_bd, st0_s, bs, ts, n_p, ts, math.gcd(ts, CHUNK_D))

        oa_s = _sample_call(
            functools.partial(_dsa_s_kernel, pp=pp, past=past, topk=topk_s, tdec=ts), "dsa_sample",
            cache_a, li, page_table, pp,
            [(p, 256, COL_AQ // 256), (p, 256, COL_AKV // 256), (p, 128, COL_AIQ // 128)],
            [cb_a, tail_a, tri_l], bs, ts, n_p,
            [pltpu.VMEM((ts, lp), I32), pltpu.VMEM((rows_dec, lp), F32), pltpu.VMEM((lp, LANES), BF16)])
        cb4 = cache_b.reshape(cache_b.shape[:3] + (2 * G_B * DH,))
        ob_s = _sample_call(
            functools.partial(_moba_s_kernel, pp=pp, past=past,
                              ntop=min(MOBA_TOPK, past // MOBA_BLOCK + 1), tdec=ts), "moba_sample",
            cb4, li, page_table, pp,
            [(p, 256, COL_BQ // 256), (p, 256, COL_BKV // 256)],
            [cb_b, tail_b], bs, ts, n_p,
            [pltpu.VMEM((rows_dec, lp), F32), pltpu.VMEM((lp, LANES), BF16),
             pltpu.VMEM((past // MOBA_BLOCK, LANES), F32)])
        oc_s = _sample_call(
            functools.partial(_mla_s_kernel, pp=pp, tdec=ts), "mla_sample",
            cache_c, li, page_table, pp,
            [(ql, H_C * KV_LORA, 0), (qr, LANES, 0), (rc, ROW_C, 0)],
            [wuvp[li]], bs, ts, n_p,
            [pltpu.VMEM((rows_dec, 1), F32), pltpu.VMEM((rows_dec, 1), F32), pltpu.VMEM((rows_dec, KV_LORA), F32)])
        st_p.append(stp)
        st_s.append(sts)

        cat = lambda a, b: jnp.concatenate([a, b], axis=0)
        x1 = _merge(x, cat(oa_p, oa_s), cat(ob_p, ob_s), cat(oc_p, oc_s), cat(od_p, od_s),
                    norm1_g[li][None, :], w_gate[li], wb[li], wo[li], tm)
        g2 = norm2_g[li][None, :]
        s0, s1, e0, e1, tau = _peer_select(x1, g2, wq[li], sk[li], _pick_tile(n, 256, LANES))
        po = _peer_dense(x1, g2, u_b[li], vt_b[li], s0, s1, e0, e1, tau, tn_peer, 8)
        x, y = _ple(x1, po, p_all[li], pg[li], ppj[li], final_norm_g[None, :], tm)

    def unstate(st, nb):
        s5 = jnp.stack(st).reshape(depth, nb, H_D, DV_D, H_D, DK_D)
        diag = jnp.stack([s5[:, :, h, :, h, :] for h in range(H_D)], axis=2)
        return jnp.swapaxes(diag, 3, 4)

    ra = jnp.stack(rows_a)
    rb = jnp.stack(rows_b)
    rcs = jnp.stack(rows_c)
    return (y[:n_p].reshape(bp, tp, d), y[n_p:].reshape(bs, ts, d),
            ra[:, :n_p].reshape(depth, bp, tp, ROW_A),
            rb[:, :n_p].reshape(depth, bp, tp, 2, G_B, DH),
            rcs[:, :n_p].reshape(depth, bp, tp, ROW_C),
            unstate(st_p, bp),
            ra[:, n_p:].reshape(depth, bs, ts, ROW_A),
            rb[:, n_p:].reshape(depth, bs, ts, 2, G_B, DH),
            rcs[:, n_p:].reshape(depth, bs, ts, ROW_C),
            unstate(st_s, bs))
```

```python
import functools
import math

import jax
import jax.numpy as jnp
import numpy as np
from jax import lax
from jax.experimental import pallas as pl
from jax.experimental.pallas import tpu as pltpu

F32 = jnp.float32
BF16 = jnp.bfloat16
I32 = jnp.int32

DH = 64
H_A = 4
H_IDX = 4
D_IDX = 32
TOPK_A_MAX = 256
H_B = 4
G_B = 2
MOBA_BLOCK = 256
MOBA_TOPK = 3
H_C = 4
Q_LORA = 192
KV_LORA = 128
D_NOPE = 64
D_ROPE = 32
D_V_C = 64
ROPE_THETA = 10000.0
H_D = 4
DK_D = 64
DV_D = 64
CHUNK_D = 16
N_BRANCH = 4
BRANCH_W = 256
N_BUCKETS = 32
MAX_DIST = 128
N_KEYS = 128
PEER_HEADS = 8
PEER_TOPK = 16
EPS = 1e-6
NEG = -1e30
F_MIN = 1e-20
ROW_A = 2 * DH + D_IDX
ROW_C = KV_LORA + D_ROPE

LANES = 128
VMEM_LIMIT = 56 * 1024 * 1024
INT_MIN = -(2 ** 31)

COL_D = 0
COL_AQ = 1024
COL_AKV = 1280
COL_BQ = 1536
COL_BKV = 1792
COL_CQ = 2048
COL_AIQ = 2304
COL_CKV = 2432
COL_CKR = 2560
P_COLS = 2688


def _cparams(*sem):
    return pltpu.CompilerParams(dimension_semantics=sem, vmem_limit_bytes=VMEM_LIMIT)


def _dot(a, b):
    return jnp.dot(a.astype(BF16), b.astype(BF16), preferred_element_type=F32)


def _dot_nt(a, b):
    return lax.dot_general(a.astype(BF16), b.astype(BF16), (((1,), (1,)), ((), ())),
                           preferred_element_type=F32)


def _split(a):
    hi = a.astype(BF16)
    lo = (a - hi.astype(F32)).astype(BF16)
    return hi, lo


def _dot3_nt(a, b):
    ah, al = _split(a)
    bh, bl = _split(b)
    dn = (((1,), (1,)), ((), ()))
    d = lambda x, y: lax.dot_general(x, y, dn, preferred_element_type=F32)
    return d(ah, bh) + (d(ah, bl) + d(al, bh))


def _rms(x, g):
    return x * lax.rsqrt(jnp.mean(x * x, axis=-1, keepdims=True) + EPS) * g


def _t5_bucket(rel):
    n = jnp.maximum(rel, 0)
    exact = N_BUCKETS // 2
    nf = jnp.maximum(n, exact).astype(F32)
    big = exact + (jnp.log(nf / exact) / math.log(MAX_DIST / exact) * (N_BUCKETS - exact)).astype(I32)
    return jnp.where(n < exact, n, jnp.minimum(big, N_BUCKETS - 1))


def _sort_key(s):
    bits = pltpu.bitcast(s, I32)
    return jnp.where(bits < 0, bits ^ jnp.int32(0x7FFFFFFF), bits)


def _lane(shape):
    return lax.broadcasted_iota(I32, shape, 1)


def _row(shape):
    return lax.broadcasted_iota(I32, shape, 0)


def _in_kernel(x_ref, g_ref, w_ref, o_ref):
    h = _rms(x_ref[...], g_ref[...])
    o_ref[...] = _dot(h, w_ref[...])


def _proj_in(x, g, w, tm):
    n, d = x.shape
    return pl.pallas_call(
        _in_kernel,
        grid=(n // tm,),
        in_specs=[pl.BlockSpec((tm, d), lambda i: (i, 0)),
                  pl.BlockSpec((1, d), lambda i: (0, 0)),
                  pl.BlockSpec((d, P_COLS), lambda i: (0, 0))],
        out_specs=pl.BlockSpec((tm, P_COLS), lambda i: (i, 0)),
        out_shape=jax.ShapeDtypeStruct((n, P_COLS), F32),
        compiler_params=_cparams("parallel"),
        name="proj_in",
    )(x, g, w)


def _rope_lanes(x, cos, sin_s):
    lane = _lane(x.shape)
    sw = jnp.where((lane % D_ROPE) < D_ROPE // 2, pltpu.roll(x, LANES - D_ROPE // 2, 1),
                   pltpu.roll(x, D_ROPE // 2, 1))
    return x * cos + sw * sin_s


def _mla_prep_kernel(cq_ref, ckv_ref, ckr_ref, cos_ref, sin_ref, cqg_ref, ckvg_ref, wuq_ref, wuk_ref,
                     rc_ref, ql_ref, qr_ref):
    cq = cq_ref[...]
    ms = jnp.sum(cq * cq, axis=-1, keepdims=True) * (1.0 / Q_LORA)
    hq = cq * lax.rsqrt(ms + EPS) * cqg_ref[...]
    qc = _dot(hq, wuq_ref[...])
    cos = cos_ref[...]
    sin_s = sin_ref[...]
    qr_ref[...] = _rope_lanes(qc[:, H_C * D_NOPE:], cos, sin_s)
    ql_ref[...] = _dot(qc[:, :H_C * D_NOPE], wuk_ref[...])
    rc_ref[:, 0:KV_LORA] = _rms(ckv_ref[...], ckvg_ref[...])
    rc_ref[:, KV_LORA:ROW_C] = _rope_lanes(ckr_ref[...], cos, sin_s)[:, 0:D_ROPE]


def _mla_prep(p, cos_t, sin_t, cqg, ckvg, wuq, wukbd, tm):
    n = p.shape[0]
    blk = lambda w, c: pl.BlockSpec((tm, w), lambda i, c=c: (i, c))
    cst = lambda a: pl.BlockSpec(a.shape, lambda i: (0,) * a.ndim)
    return pl.pallas_call(
        _mla_prep_kernel,
        grid=(n // tm,),
        in_specs=[blk(256, COL_CQ // 256), blk(128, COL_CKV // 128), blk(128, COL_CKR // 128),
                  blk(128, 0), blk(128, 0), cst(cqg), cst(ckvg), cst(wuq), cst(wukbd)],
        out_specs=[pl.BlockSpec((tm, ROW_C), lambda i: (i, 0)),
                   pl.BlockSpec((tm, H_C * KV_LORA), lambda i: (i, 0)),
                   pl.BlockSpec((tm, LANES), lambda i: (i, 0))],
        out_shape=[jax.ShapeDtypeStruct((n, ROW_C), F32),
                   jax.ShapeDtypeStruct((n, H_C * KV_LORA), F32),
                   jax.ShapeDtypeStruct((n, LANES), F32)],
        compiler_params=_cparams("parallel"),
        name="mla_prep",
    )(p, p, p, cos_t, sin_t, cqg, ckvg, wuq, wukbd)


def _merge_kernel(x_ref, oa_ref, ob_ref, oc_ref, od_ref, g_ref, wg_ref, wb_ref, wo_ref, o_ref):
    x = x_ref[...]
    h = _rms(x, g_ref[...]).astype(BF16)
    d = x.shape[1]
    mix = None
    for m, o_m in enumerate((oa_ref, ob_ref, oc_ref, od_ref)):
        gate = jax.nn.sigmoid(jnp.dot(h, wg_ref[:, m * d:(m + 1) * d], preferred_element_type=F32))
        term = gate * _dot(o_m[...], wb_ref[m])
        mix = term if mix is None else mix + term
    o_ref[...] = x + _dot(mix, wo_ref[...])


def _merge(x, oa, ob, oc, od, g, wg, wb, wo, tm):
    n, d = x.shape
    row = lambda w: pl.BlockSpec((tm, w), lambda i: (i, 0))
    cst = lambda a: pl.BlockSpec(a.shape, lambda i: (0,) * a.ndim)
    return pl.pallas_call(
        _merge_kernel,
        grid=(n // tm,),
        in_specs=[row(d), row(BRANCH_W), row(BRANCH_W), row(BRANCH_W), row(BRANCH_W),
                  cst(g), cst(wg), cst(wb), cst(wo)],
        out_specs=row(d),
        out_shape=jax.ShapeDtypeStruct((n, d), F32),
        compiler_params=_cparams("parallel"),
        name="merge",
    )(x, oa, ob, oc, od, g, wg, wb, wo)


def _ple_kernel(x_ref, po_ref, p_ref, pg_ref, pp_ref, gf_ref, o_ref, y_ref):
    x2 = x_ref[...] + po_ref[...]
    xn = x2 + jax.nn.sigmoid(_dot(x2, pg_ref[...])) * _dot(p_ref[...], pp_ref[...])
    o_ref[...] = xn
    y_ref[...] = _rms(xn, gf_ref[...])


def _ple(x, po, p, pg, pp, gf, tm):
    n, d = x.shape
    row = lambda w: pl.BlockSpec((tm, w), lambda i: (i, 0))
    cst = lambda a: pl.BlockSpec(a.shape, lambda i: (0,) * a.ndim)
    return pl.pallas_call(
        _ple_kernel,
        grid=(n // tm,),
        in_specs=[row(d), row(d), row(p.shape[1]), cst(pg), cst(pp), cst(gf)],
        out_specs=[row(d), row(d)],
        out_shape=[jax.ShapeDtypeStruct((n, d), F32), jax.ShapeDtypeStruct((n, d), F32)],
        compiler_params=_cparams("parallel"),
        name="ple",
    )(x, po, p, pg, pp, gf)


PEER_NCAND = 80
PEER_NTOP = 24


def _extract_top(w_s, out_s, n_chains, count):
    def body(r, carry):
        for c in range(n_chains):
            w = w_s[c]
            m = jnp.max(w, axis=0, keepdims=True)
            out_s[c, pl.ds(r, 1), :] = m
            w_s[c] = jnp.where(w == m, -jnp.inf, w)
        return carry
    lax.fori_loop(0, count, body, 0)


def _peer_sel_kernel(x_ref, g_ref, wq_ref, sk_ref, th_ref, s1_ref, e0_ref, e1_ref, w_s, sv_s, c_s, top_s):
    hn = _rms(x_ref[...], g_ref[...])
    q = _dot(hn, wq_ref[...])
    for hp in range(2 * PEER_HEADS):
        s = _dot_nt(sk_ref[hp], q[:, hp * N_KEYS:(hp + 1) * N_KEYS])
        w_s[hp] = s
        if hp % 2 == 0:
            th_ref[hp // 2] = s
        else:
            s1_ref[hp // 2] = s
    _extract_top(w_s, sv_s, 2 * PEER_HEADS, PEER_TOPK)
    half = PEER_TOPK // 2
    for h in range(PEER_HEADS):
        sv0 = sv_s[2 * h]
        sv1 = sv_s[2 * h + 1]
        pieces = [sv0[0:1, :] + sv1]
        pieces += [sv0[a:a + 1, :] + sv1[0:half, :] for a in range(1, half)]
        pieces.append(sv0[half:PEER_TOPK, :] + sv1[0:1, :])
        c_s[h] = jnp.concatenate(pieces, axis=0)
    _extract_top(c_s, top_s, PEER_HEADS, PEER_TOPK + 1)
    for h in range(PEER_HEADS):
        top = top_s[h]
        z = jnp.sum(jnp.exp(top[0:PEER_TOPK, :] - top[0:1, :]), axis=0, keepdims=True)
        cut = 0.5 * (top[PEER_TOPK - 1:PEER_TOPK, :] + top[PEER_TOPK:PEER_TOPK + 1, :])
        s0 = th_ref[h]
        e0_ref[h] = jnp.exp(s0 - sv_s[2 * h, 0:1, :]) / z
        e1_ref[h] = jnp.exp(s1_ref[h] - sv_s[2 * h + 1, 0:1, :])
        th_ref[h] = cut - s0


def _peer_select(x, g, wq, sk, tn):
    n, d = x.shape
    hk = jax.ShapeDtypeStruct((PEER_HEADS, N_KEYS, n), F32)
    hk_spec = pl.BlockSpec((PEER_HEADS, N_KEYS, tn), lambda i: (0, 0, i))
    cst = lambda a: pl.BlockSpec(a.shape, lambda i: (0,) * a.ndim)
    return pl.pallas_call(
        _peer_sel_kernel,
        grid=(n // tn,),
        in_specs=[pl.BlockSpec((tn, d), lambda i: (i, 0)), cst(g), cst(wq), cst(sk)],
        out_specs=[hk_spec, hk_spec, hk_spec, hk_spec],
        out_shape=[hk, hk, hk, hk],
        scratch_shapes=[pltpu.VMEM((2 * PEER_HEADS, N_KEYS, tn), F32),
                        pltpu.VMEM((2 * PEER_HEADS, PEER_TOPK, tn), F32),
                        pltpu.VMEM((PEER_HEADS, PEER_NCAND, tn), F32),
                        pltpu.VMEM((PEER_HEADS, PEER_NTOP, tn), F32)],
        compiler_params=_cparams("parallel"),
        name="peer_select",
    )(x, g, wq, sk)


def _peer_dense_kernel(x_ref, g_ref, u_ref, vt_ref, th_ref, s1_ref, e0_ref, e1_ref, o_ref, hn_s, acc_s,
                       act_s, z_s, *, ic):
    c = pl.program_id(1)
    tn = hn_s.shape[0]

    @pl.when(c == 0)
    def _():
        hn_s[...] = _rms(x_ref[...], g_ref[...]).astype(BF16)
        acc_s[...] = jnp.zeros_like(acc_s)
        act_s[...] = jnp.zeros_like(act_s)
        z_s[...] = jnp.zeros_like(z_s)

    cur = c % 2
    prv = 1 - cur
    acc_s[...] += jnp.dot(vt_ref[...], z_s[cur], preferred_element_type=F32)
    for ii in range(ic):
        for lt in range(tn // LANES):
            tok = slice(lt * LANES, (lt + 1) * LANES)
            wt = None
            for h in range(PEER_HEADS):
                t = jnp.where(s1_ref[h, :, tok] >= th_ref[h, ii:ii + 1, tok], e1_ref[h, :, tok], 0.0)
                term = e0_ref[h, ii:ii + 1, tok] * t
                wt = term if wt is None else wt + term
            a = jax.nn.gelu(act_s[prv, ii * N_KEYS:(ii + 1) * N_KEYS, tok])
            z_s[prv, ii * N_KEYS:(ii + 1) * N_KEYS, tok] = (wt * a).astype(BF16)
    act_s[cur] = lax.dot_general(u_ref[...], hn_s[...], (((1,), (1,)), ((), ())),
                                 preferred_element_type=F32)

    @pl.when(c == pl.num_programs(1) - 1)
    def _():
        o_ref[...] = acc_s[...].T


def _peer_dense(x, g, u, vt, th, s1, e0, e1, tn, ic):
    n, d = x.shape
    ne = u.shape[0]
    ec = ic * N_KEYS
    nc = ne // ec
    clamp = lambda c: jnp.minimum(jnp.maximum(c, 0), nc - 1)
    return pl.pallas_call(
        functools.partial(_peer_dense_kernel, ic=ic),
        grid=(n // tn, nc + 2),
        in_specs=[pl.BlockSpec((tn, d), lambda i, c: (i, 0)),
                  pl.BlockSpec((1, d), lambda i, c: (0, 0)),
                  pl.BlockSpec((ec, d), lambda i, c: (clamp(c), 0)),
                  pl.BlockSpec((d, ec), lambda i, c: (0, clamp(c - 2))),
                  pl.BlockSpec((PEER_HEADS, ic, tn), lambda i, c: (0, clamp(c - 1), i)),
                  pl.BlockSpec((PEER_HEADS, N_KEYS, tn), lambda i, c: (0, 0, i)),
                  pl.BlockSpec((PEER_HEADS, ic, tn), lambda i, c: (0, clamp(c - 1), i)),
                  pl.BlockSpec((PEER_HEADS, N_KEYS, tn), lambda i, c: (0, 0, i))],
        out_specs=pl.BlockSpec((tn, d), lambda i, c: (i, 0)),
        out_shape=jax.ShapeDtypeStruct((n, d), F32),
        scratch_shapes=[pltpu.VMEM((tn, d), BF16), pltpu.VMEM((d, tn), F32),
                        pltpu.VMEM((2, ec, tn), F32), pltpu.VMEM((2, ec, tn), BF16)],
        compiler_params=_cparams("parallel", "arbitrary"),
        name="peer_dense",
    )(x, g, u, vt, th, s1, e0, e1)


def _hgrn_kernel(p_ref, lb_ref, dg_ref, ones_ref, st0_ref, o_ref, st_ref, q_s, k_s, b_s, v_s, o_s,
                 *, ts, ch):
    s = pl.program_id(1)
    w = H_D * DK_D

    @pl.when(s == 0)
    def _():
        st_ref[...] = st0_ref[...]

    dq = p_ref[:, 0:w]
    df = p_ref[:, w:2 * w]
    lb = lb_ref[...]
    sig = jax.nn.sigmoid(df)
    logf = jnp.log(jnp.maximum(lb + (1.0 - lb) * sig, F_MIN))
    r = _row((ts, w)) % ch
    b = logf
    sh = 1
    while sh < ch:
        b = b + jnp.where(r >= sh, pltpu.roll(b, sh, 0), 0.0)
        sh *= 2
    q_s[...] = dq * jax.nn.sigmoid(dq)
    k_s[...] = (1.0 - lb) * (1.0 - sig)
    b_s[...] = b
    v_s[...] = p_ref[:, 2 * w:3 * w]
    ones_bd = ones_ref[...]
    bd_mask = ones_bd.astype(F32)
    srow = _row((ch, w))

    def body(c, carry):
        off = pl.multiple_of(c * ch, ch)
        q = q_s[pl.ds(off, ch), :]
        kk = k_s[pl.ds(off, ch), :]
        bb = b_s[pl.ds(off, ch), :]
        v = v_s[pl.ds(off, ch), :]
        st = st_ref[0]
        o_inter = _dot_nt(q * jnp.exp(bb), st)
        rows = []
        for t in range(ch):
            dec = jnp.exp(jnp.where(srow <= t, bb[t:t + 1, :] - bb, 0.0))
            rows.append(jnp.where(srow <= t, dec * kk * q[t:t + 1, :], 0.0))
        pm = jnp.concatenate(rows, axis=0)
        rr = jnp.dot(pm.astype(BF16), ones_bd, preferred_element_type=F32)
        o_intra = jnp.sum(rr.reshape(ch, ch, w) * v[None, :, :], axis=1)
        o_s[pl.ds(off, ch), :] = o_inter + o_intra
        bl = bb[ch - 1:ch, :]
        kdec = kk * jnp.exp(bl - bb)
        upd = lax.dot_general(v.astype(BF16), kdec.astype(BF16), (((0,), (0,)), ((), ())),
                              preferred_element_type=F32)
        st_ref[0] = (st * jnp.exp(bl) + upd) * bd_mask
        return carry

    lax.fori_loop(0, ts // ch, body, 0)
    o = o_s[...]
    ms = jnp.dot((o * o).astype(BF16), ones_bd, preferred_element_type=F32) * (1.0 / DV_D)
    dg = p_ref[:, 3 * w:4 * w]
    o_ref[...] = o * lax.rsqrt(ms + EPS) * dg_ref[...] * (dg * jax.nn.sigmoid(dg))


def _hgrn(p, lb, dg, ones_bd, st0, nb, t, row0, ts, ch):
    w = H_D * DK_D
    ns = t // ts
    rb0 = row0 // ts
    cst = lambda a: pl.BlockSpec(a.shape, lambda b, s: (0,) * a.ndim)
    return pl.pallas_call(
        functools.partial(_hgrn_kernel, ts=ts, ch=ch),
        grid=(nb, ns),
        in_specs=[pl.BlockSpec((ts, 4 * w), lambda b, s: (rb0 + b * ns + s, COL_D // (4 * w))),
                  cst(lb), cst(dg), cst(ones_bd),
                  pl.BlockSpec((1, w, w), lambda b, s: (b, 0, 0))],
        out_specs=[pl.BlockSpec((ts, w), lambda b, s: (b * ns + s, 0)),
                   pl.BlockSpec((1, w, w), lambda b, s: (b, 0, 0))],
        out_shape=[jax.ShapeDtypeStruct((nb * t, w), F32), jax.ShapeDtypeStruct((nb, w, w), F32)],
        scratch_shapes=[pltpu.VMEM((ts, w), F32)] * 5,
        compiler_params=_cparams("parallel", "arbitrary"),
        name="hgrn",
    )(p, lb, dg, ones_bd, st0)


def _radix_threshold(count_ge, k, rows):
    def body(i, t):
        cand = t ^ jnp.left_shift(jnp.int32(1), 31 - i)
        return jnp.where(count_ge(cand) >= k, cand, t)
    return lax.fori_loop(0, 32, body, jnp.full((rows, 1), INT_MIN, I32))


def _head_rows_low(q, scale):
    lane = _lane((q.shape[0], LANES))
    out = []
    for m in range(2):
        tile = q[:, m * LANES:(m + 1) * LANES] * scale
        out.append(jnp.where(lane < DH, tile, 0.0))
        out.append(jnp.where(lane < DH, pltpu.roll(tile, DH, 1), 0.0))
    return jnp.concatenate(out, axis=0)


def _head_rows_group(q, scale):
    lane = _lane((q.shape[0], LANES))
    t0 = q[:, 0:LANES] * scale
    t1 = q[:, LANES:2 * LANES] * scale
    return jnp.concatenate([
        jnp.where(lane < DH, t0, 0.0),
        jnp.where(lane < DH, pltpu.roll(t0, DH, 1), 0.0),
        jnp.where(lane >= DH, pltpu.roll(t1, DH, 1), 0.0),
        jnp.where(lane >= DH, t1, 0.0)], axis=0)


def _idx_heads(iq):
    lane = _lane(iq.shape)
    return [jnp.where(lane < D_IDX, iq if h == 0 else pltpu.roll(iq, LANES - D_IDX * h, 1), 0.0)
            for h in range(H_IDX)]


def _index_score(iqh, iw, ik_tile):
    s = None
    for h in range(H_IDX):
        r = jnp.maximum(_dot3_nt(iqh[h], ik_tile), 0.0)
        term = iw[:, h:h + 1] * r
        s = term if s is None else s + term
    return s * ((H_IDX * D_IDX) ** -0.5) + 0.0


def _softmax_step(lg, pv, m_i, l_i, acc):
    m_new = jnp.maximum(m_i, jnp.max(lg, axis=1, keepdims=True))
    alpha = jnp.exp(m_i - m_new)
    p = jnp.exp(lg - m_new)
    l_new = alpha * l_i + jnp.sum(p, axis=1, keepdims=True)
    return m_new, l_new, alpha * acc + pv(p)


def _dsa_p_kernel(q_ref, qkv_ref, iq_ref, kv_ref, bt_ref, tri_ref, o_ref, key_s, *, tq, chk, topk):
    qi = pl.program_id(1)
    nch = (qi * tq + tq + chk - 1) // chk
    qpos = qi * tq + _row((tq, 1))
    iw = qkv_ref[:, ROW_A:ROW_A + H_IDX]
    iqh = _idx_heads(iq_ref[...])
    nv = bt_ref.shape[0]

    def kpos_of(off):
        return off + _lane((tq, chk))

    def score_body(c, carry):
        off = pl.multiple_of(c * chk, chk)
        s = _index_score(iqh, iw, kv_ref[pl.ds(off, chk), LANES:2 * LANES])
        s = jnp.where(kpos_of(off) <= qpos, s, NEG)
        key_s[:, pl.ds(off, chk)] = _sort_key(s)
        return carry

    lax.fori_loop(0, nch, score_body, 0)

    def count(pred):
        def body(c, acc):
            off = pl.multiple_of(c * chk, chk)
            m = pred(key_s[:, pl.ds(off, chk)]).astype(I32)
            for l in range(chk // LANES):
                acc = acc + m[:, l * LANES:(l + 1) * LANES]
            return acc
        acc = lax.fori_loop(0, nch, body, jnp.zeros((tq, LANES), I32))
        return jnp.sum(acc, axis=1, keepdims=True)

    thr = _radix_threshold(lambda cand: count(lambda kc: kc >= cand), topk, tq)
    need = (topk - count(lambda kc: kc > thr)).astype(F32)

    qs = _head_rows_low(q_ref[...], DH ** -0.5).astype(BF16)

    def att_body(c, carry):
        m_i, l_i, acc, run = carry
        off = pl.multiple_of(c * chk, chk)
        kc = key_s[:, pl.ds(off, chk)]
        eq = kc == thr
        eqf = jnp.where(eq, 1.0, 0.0)
        pref = jnp.dot(eqf.astype(BF16), tri_ref[...], preferred_element_type=F32) + run
        sel = ((kc > thr) | (eq & (pref < need))) & (kpos_of(off) <= qpos)
        kvt = kv_ref[pl.ds(off, chk), 0:LANES].astype(BF16)
        lg = lax.dot_general(qs, kvt, (((1,), (1,)), ((), ())), preferred_element_type=F32)
        r = jnp.minimum(qi * (tq // LANES) - c * (chk // LANES), nv - 1)
        lg = lg + bt_ref[r]
        lg = lg + jnp.concatenate([jnp.where(sel, 0.0, NEG)] * H_A, axis=0)
        m_i, l_i, acc = _softmax_step(lg, lambda p: _dot(p, kvt), m_i, l_i, acc)
        return m_i, l_i, acc, run + jnp.sum(eqf, axis=1, keepdims=True)

    init = (jnp.full((H_A * tq, 1), -jnp.inf, F32), jnp.zeros((H_A * tq, 1), F32),
            jnp.zeros((H_A * tq, LANES), F32), jnp.zeros((tq, 1), F32))
    _, l_i, acc, _ = lax.fori_loop(0, nch, att_body, init)
    o = acc / l_i
    lane = _lane((tq, LANES))
    for m in range(2):
        o_ref[:, m * LANES:(m + 1) * LANES] = jnp.where(
            lane < DH, pltpu.roll(o[2 * m * tq:(2 * m + 1) * tq], DH, 1), o[(2 * m + 1) * tq:(2 * m + 2) * tq])


def _dsa_prompt(p, bt, tri, nb, t, tq, chk, topk):
    nq = t // tq
    return pl.pallas_call(
        functools.partial(_dsa_p_kernel, tq=tq, chk=chk, topk=topk),
        grid=(nb, nq),
        in_specs=[pl.BlockSpec((tq, 256), lambda b, i: (b * nq + i, COL_AQ // 256)),
                  pl.BlockSpec((tq, 256), lambda b, i: (b * nq + i, COL_AKV // 256)),
                  pl.BlockSpec((tq, 128), lambda b, i: (b * nq + i, COL_AIQ // 128)),
                  pl.BlockSpec((t, 256), lambda b, i: (b, COL_AKV // 256)),
                  pl.BlockSpec(bt.shape, lambda b, i: (0, 0, 0)),
                  pl.BlockSpec(tri.shape, lambda b, i: (0, 0))],
        out_specs=pl.BlockSpec((tq, BRANCH_W), lambda b, i: (b * nq + i, 0)),
        out_shape=jax.ShapeDtypeStruct((nb * t, BRANCH_W), F32),
        scratch_shapes=[pltpu.VMEM((tq, t), I32)],
        compiler_params=_cparams("parallel", "arbitrary"),
        name="dsa_prompt",
    )(p, p, p, p, bt, tri)


def _moba_p_kernel(q_ref, kv_ref, bt_ref, o_ref, m_s, l_s, acc_s, *, nblk, ntop):
    qi = pl.program_id(1)
    tq = MOBA_BLOCK
    rows = H_B * tq
    qs = _head_rows_group(q_ref[...], DH ** -0.5)
    kmean = jnp.sum(kv_ref[:, 0:LANES].reshape(nblk, MOBA_BLOCK, LANES), axis=1) * (1.0 / MOBA_BLOCK)
    gate = _dot3_nt(qs, kmean)
    n_l = _lane((rows, nblk))
    gate = jnp.where(n_l < qi, gate, NEG)
    rank = jnp.zeros((rows, nblk), I32)
    for m in range(nblk):
        col = gate[:, m:m + 1]
        beats = (col > gate) | ((col == gate) & (m < n_l))
        rank = rank + beats.astype(I32)
    bm = ((rank < ntop) & (gate > 0.5 * NEG)) | (n_l == qi)
    bmf = jnp.where(bm, 1.0, 0.0)
    m_s[...] = jnp.full(m_s.shape, -jnp.inf, F32)
    l_s[...] = jnp.zeros(l_s.shape, F32)
    acc_s[...] = jnp.zeros(acc_s.shape, F32)
    qb = qs.astype(BF16)
    qpos = qi * tq + _row((tq, 1))
    qpos4 = jnp.concatenate([qpos] * H_B, axis=0)
    for n in range(nblk):
        @pl.when(n <= qi)
        def _(n=n):
            kt = kv_ref[n * MOBA_BLOCK:(n + 1) * MOBA_BLOCK, 0:LANES].astype(BF16)
            vt = kv_ref[n * MOBA_BLOCK:(n + 1) * MOBA_BLOCK, LANES:2 * LANES].astype(BF16)
            lg = lax.dot_general(qb, kt, (((1,), (1,)), ((), ())), preferred_element_type=F32)
            lg = lg + bt_ref[jnp.minimum(qi - n, bt_ref.shape[0] - 1)]
            kpos = n * MOBA_BLOCK + _lane((rows, MOBA_BLOCK))
            ok = (bmf[:, n:n + 1] > 0.5) & (kpos <= qpos4)
            lg = jnp.where(ok, lg, NEG)
            m_i, l_i, acc = _softmax_step(lg, lambda p: _dot(p, vt), m_s[...], l_s[...], acc_s[...])
            m_s[...] = m_i
            l_s[...] = l_i
            acc_s[...] = acc
    o = acc_s[...] / l_s[...]
    lane = _lane((tq, LANES))
    o_ref[:, 0:LANES] = jnp.where(lane < DH, o[0:tq], pltpu.roll(o[tq:2 * tq], DH, 1))
    o_ref[:, LANES:2 * LANES] = jnp.where(lane < DH, pltpu.roll(o[2 * tq:3 * tq], DH, 1), o[3 * tq:4 * tq])


def _moba_prompt(p, bt, nb, t):
    tq = MOBA_BLOCK
    nq = t // tq
    rows = H_B * tq
    return pl.pallas_call(
        functools.partial(_moba_p_kernel, nblk=nq, ntop=min(MOBA_TOPK, nq)),
        grid=(nb, nq),
        in_specs=[pl.BlockSpec((tq, 256), lambda b, i: (b * nq + i, COL_BQ // 256)),
                  pl.BlockSpec((t, 256), lambda b, i: (b, COL_BKV // 256)),
                  pl.BlockSpec(bt.shape, lambda b, i: (0, 0, 0))],
        out_specs=pl.BlockSpec((tq, BRANCH_W), lambda b, i: (b * nq + i, 0)),
        out_shape=jax.ShapeDtypeStruct((nb * t, BRANCH_W), F32),
        scratch_shapes=[pltpu.VMEM((rows, 1), F32), pltpu.VMEM((rows, 1), F32), pltpu.VMEM((rows, LANES), F32)],
        compiler_params=_cparams("parallel", "arbitrary"),
        name="moba_prompt",
    )(p, p, bt)


def _mla_q_rows(ql, qr):
    qlat = jnp.concatenate([ql[:, h * KV_LORA:(h + 1) * KV_LORA] for h in range(H_C)], axis=0)
    qrope = jnp.concatenate([qr[:, h * D_ROPE:(h + 1) * D_ROPE] for h in range(H_C)], axis=0)
    return qlat.astype(BF16), qrope.astype(BF16)


def _mla_out(o_lat, wuv_ref, t):
    out = None
    for h in range(H_C):
        term = _dot(o_lat[h * t:(h + 1) * t], wuv_ref[h])
        out = term if out is None else out + term
    return out


def _mla_p_kernel(ql_ref, qr_ref, rc_ref, wuv_ref, o_ref, *, tq, chk):
    qi = pl.program_id(1)
    nch = (qi * tq + tq + chk - 1) // chk
    rows = H_C * tq
    qlat, qrope = _mla_q_rows(ql_ref[...], qr_ref[...])
    qpos = qi * tq + _row((tq, 1))
    qpos4 = jnp.concatenate([qpos] * H_C, axis=0)
    scale = (D_NOPE + D_ROPE) ** -0.5
    dn = (((1,), (1,)), ((), ()))

    def body(c, carry):
        m_i, l_i, acc = carry
        off = pl.multiple_of(c * chk, chk)
        ckv = rc_ref[pl.ds(off, chk), 0:KV_LORA].astype(BF16)
        kr = rc_ref[pl.ds(off, chk), KV_LORA:ROW_C].astype(BF16)
        s = (lax.dot_general(qlat, ckv, dn, preferred_element_type=F32)
             + lax.dot_general(qrope, kr, dn, preferred_element_type=F32)) * scale
        kpos = off + _lane((rows, chk))
        s = jnp.where(kpos <= qpos4, s, NEG)
        return _softmax_step(s, lambda p: _dot(p, ckv), m_i, l_i, acc)

    init = (jnp.full((rows, 1), -jnp.inf, F32), jnp.zeros((rows, 1), F32), jnp.zeros((rows, KV_LORA), F32))
    _, l_i, acc = lax.fori_loop(0, nch, body, init)
    o_ref[...] = _mla_out(acc / l_i, wuv_ref, tq)


def _mla_prompt(ql, qr, rc, wuvp, nb, t, tq, chk):
    nq = t // tq
    return pl.pallas_call(
        functools.partial(_mla_p_kernel, tq=tq, chk=chk),
        grid=(nb, nq),
        in_specs=[pl.BlockSpec((tq, H_C * KV_LORA), lambda b, i: (b * nq + i, 0)),
                  pl.BlockSpec((tq, LANES), lambda b, i: (b * nq + i, 0)),
                  pl.BlockSpec((t, ROW_C), lambda b, i: (b, 0)),
                  pl.BlockSpec(wuvp.shape, lambda b, i: (0, 0, 0))],
        out_specs=pl.BlockSpec((tq, BRANCH_W), lambda b, i: (b * nq + i, 0)),
        out_shape=jax.ShapeDtypeStruct((nb * t, BRANCH_W), F32),
        compiler_params=_cparams("parallel", "arbitrary"),
        name="mla_prompt",
    )(ql, qr, rc, wuvp)


def _page_specs(shape_tail, li, pp):
    nd = len(shape_tail)
    return [pl.BlockSpec((None, None) + shape_tail,
                         lambda b, s, pt, j=j: (li, pt[b, s * pp + j]) + (0,) * nd)
            for j in range(pp)]


def _cat_pages(pages, lo, hi):
    return jnp.concatenate([pg[lo:hi, :] for pg in pages], axis=1)


def _pad_rows(x, rows):
    return jnp.concatenate([x, jnp.zeros((rows - x.shape[0], x.shape[1]), x.dtype)], axis=0)


def _head_stack(x, width):
    return jnp.concatenate([x[:, h * width:(h + 1) * width] for h in range(x.shape[1] // width)], axis=0)


def _head_unstack(o, t):
    return jnp.concatenate([o[h * t:(h + 1) * t] for h in range(o.shape[0] // t)], axis=1)


def _dot3(a, b):
    ah, al = _split(a)
    bh, bl = _split(b)
    d = lambda x, y: jnp.dot(x, y, preferred_element_type=F32)
    return d(ah, bh) + (d(ah, bl) + d(al, bh))


def _tail_bias(lg, cb_ref, tail_ref):
    lg = lg + cb_ref[...]
    w = lg.shape[1] - 2 * LANES
    return jnp.concatenate([lg[:, :w], lg[:, w:] + tail_ref[...]], axis=1)


def _dsa_s_kernel(pt_ref, *refs, pp, past, topk, tdec):
    pages = refs[:pp]
    q_ref, qkv_ref, iq_ref, cb_ref, tail_ref, tri_ref, o_ref, key_s, lg_s, vt_s = refs[pp:]
    s = pl.program_id(1)
    lp = past + LANES
    wid = pp * LANES
    iw = qkv_ref[:, ROW_A:ROW_A + H_IDX]
    iqs = _head_stack(iq_ref[...], D_IDX)
    qs = (_head_stack(q_ref[...], DH) * DH ** -0.5).astype(BF16)

    def combine(r):
        sc = None
        for h in range(H_IDX):
            term = iw[:, h:h + 1] * r[h * tdec:(h + 1) * tdec]
            sc = term if sc is None else sc + term
        return sc * ((H_IDX * D_IDX) ** -0.5) + 0.0

    off = pl.multiple_of(s * wid, wid)
    ikt = _cat_pages(pages, 2 * DH, ROW_A)
    key_s[:, pl.ds(off, wid)] = _sort_key(combine(jnp.maximum(_dot3(iqs, ikt), 0.0)))
    lg_s[:, pl.ds(off, wid)] = jnp.dot(qs, _cat_pages(pages, 0, DH).astype(BF16), preferred_element_type=F32)
    vt_s[:, pl.ds(off, wid)] = _cat_pages(pages, DH, 2 * DH).astype(BF16)

    @pl.when(s == pl.num_programs(1) - 1)
    def _():
        knew = _pad_rows(qkv_ref[:, 0:DH], LANES)
        vnew = _pad_rows(qkv_ref[:, DH:2 * DH], LANES)
        iknew = _pad_rows(qkv_ref[:, 2 * DH:ROW_A], LANES)
        sc = combine(jnp.maximum(_dot3_nt(iqs, iknew), 0.0))
        causal = _lane((tdec, LANES)) <= _row((tdec, LANES))
        key_s[:, past:lp] = _sort_key(jnp.where(causal, sc, NEG))
        lg_s[:, past:lp] = _dot_nt(qs, knew)
        keys = key_s[...]
        thr = _radix_threshold(
            lambda cand: jnp.sum((keys >= cand).astype(I32), axis=1, keepdims=True), topk, tdec)
        need = (topk - jnp.sum((keys > thr).astype(I32), axis=1, keepdims=True)).astype(F32)
        eq = keys == thr
        eqf = jnp.where(eq, 1.0, 0.0)
        nck = lp // LANES
        stacked = jnp.concatenate([eqf[:, c * LANES:(c + 1) * LANES] for c in range(nck)],
                                  axis=0).astype(BF16)
        pin = jnp.dot(stacked, tri_ref[...], preferred_element_type=F32)
        tot = jnp.dot(stacked, jnp.ones((LANES, LANES), BF16), preferred_element_type=F32)
        run = jnp.zeros((tdec, LANES), F32)
        pref = []
        for c in range(nck):
            pref.append(pin[c * tdec:(c + 1) * tdec] + run)
            run = run + tot[c * tdec:(c + 1) * tdec]
        pref = jnp.concatenate(pref, axis=1)
        kpos = _lane((tdec, lp))
        sel = ((keys > thr) | (eq & (pref < need))) & ((kpos < past) | (kpos - past <= _row((tdec, lp))))
        lg = _tail_bias(lg_s[...], cb_ref, tail_ref)
        lg = lg + jnp.concatenate([jnp.where(sel, 0.0, NEG)] * H_A, axis=0)
        p = jnp.exp(lg - jnp.max(lg, axis=1, keepdims=True))
        o = (_dot_nt(p[:, :past], vt_s[...]) + _dot(p[:, past:], vnew)) / jnp.sum(p, axis=1, keepdims=True)
        o_ref[...] = _head_unstack(o, tdec)


def _sample_call(kernel, name, cache, li, pt, pp, row_inputs, const_inputs, nb, tdec, row0, scratch):
    npages = pt.shape[1]
    rb0 = row0 // tdec
    in_specs = _page_specs(cache.shape[2:], li, pp)
    args = [cache] * pp
    for arr, width, col in row_inputs:
        in_specs.append(pl.BlockSpec((tdec, width), lambda b, s, pt, col=col: (rb0 + b, col)))
        args.append(arr)
    for arr in const_inputs:
        in_specs.append(pl.BlockSpec(arr.shape, lambda b, s, pt, nd=arr.ndim: (0,) * nd))
        args.append(arr)
    return pl.pallas_call(
        kernel,
        grid_spec=pltpu.PrefetchScalarGridSpec(
            num_scalar_prefetch=1,
            grid=(nb, npages // pp),
            in_specs=in_specs,
            out_specs=pl.BlockSpec((tdec, BRANCH_W), lambda b, s, pt: (b, 0)),
            scratch_shapes=scratch),
        out_shape=jax.ShapeDtypeStruct((nb * tdec, BRANCH_W), F32),
        compiler_params=_cparams("parallel", "arbitrary"),
        name=name,
    )(pt, *args)


def _moba_s_kernel(pt_ref, *refs, pp, past, ntop, tdec):
    pages = refs[:pp]
    q_ref, kv_ref, cb_ref, tail_ref, o_ref, lg_s, vt_s, ks_s = refs[pp:]
    s = pl.program_id(1)
    lp = past + LANES
    wid = pp * LANES
    nbk = past // MOBA_BLOCK
    ppb = MOBA_BLOCK // LANES
    gw = G_B * DH
    grows = (H_B // G_B) * tdec
    rows = H_B * tdec
    qf = _head_stack(q_ref[...], DH) * DH ** -0.5
    qb = qf.astype(BF16)

    @pl.when(s == 0)
    def _():
        ks_s[...] = jnp.zeros_like(ks_s)

    off = pl.multiple_of(s * wid, wid)
    for g in range(G_B):
        ktg = _cat_pages(pages, g * DH, (g + 1) * DH).astype(BF16)
        lg_s[g * grows:(g + 1) * grows, pl.ds(off, wid)] = jnp.dot(
            qb[g * grows:(g + 1) * grows], ktg, preferred_element_type=F32)
    vt_s[:, pl.ds(off, wid)] = _cat_pages(pages, gw, 2 * gw).astype(BF16)
    blane = _lane(ks_s.shape)
    ksum = ks_s[...]
    for jb in range(pp // ppb):
        kt = pages[jb * ppb][0:gw, :]
        for j in range(1, ppb):
            kt = kt + pages[jb * ppb + j][0:gw, :]
        ksum = ksum + jnp.where(blane == s * (pp // ppb) + jb, jnp.sum(kt, axis=1, keepdims=True), 0.0)
    ks_s[...] = ksum

    @pl.when(s == pl.num_programs(1) - 1)
    def _():
        knew = kv_ref[:, 0:gw]
        vnew = kv_ref[:, gw:2 * gw]
        kmean = ks_s[...] * (1.0 / MOBA_BLOCK)
        gate = jnp.concatenate([_dot3(qf[g * grows:(g + 1) * grows], kmean[g * DH:(g + 1) * DH])
                                for g in range(G_B)], axis=0)
        n_l = _lane(gate.shape)
        gate = jnp.where(n_l < nbk, gate, NEG)
        rank = jnp.zeros(gate.shape, I32)
        for m in range(nbk):
            col = gate[:, m:m + 1]
            rank = rank + ((col > gate) | ((col == gate) & (m < n_l))).astype(I32)
        bneg = jnp.where((rank < ntop) & (gate > 0.5 * NEG), 0.0, NEG)
        trow = jnp.concatenate([_row((tdec, LANES))] * H_B, axis=0)
        neg_mask = jnp.concatenate(
            [jnp.broadcast_to(bneg[:, n:n + 1], (rows, MOBA_BLOCK)) for n in range(nbk)]
            + [jnp.where(_lane((rows, LANES)) <= trow, 0.0, NEG)], axis=1)
        for g in range(G_B):
            lg_s[g * grows:(g + 1) * grows, past:lp] = _dot_nt(
                qb[g * grows:(g + 1) * grows], _pad_rows(knew[:, g * DH:(g + 1) * DH], LANES))
        lg = _tail_bias(lg_s[...], cb_ref, tail_ref) + neg_mask
        p = jnp.exp(lg - jnp.max(lg, axis=1, keepdims=True))
        o = jnp.concatenate(
            [_dot_nt(p[g * grows:(g + 1) * grows, :past], vt_s[g * DH:(g + 1) * DH, :])
             + _dot(p[g * grows:(g + 1) * grows, past:], _pad_rows(vnew[:, g * DH:(g + 1) * DH], LANES))
             for g in range(G_B)], axis=0) / jnp.sum(p, axis=1, keepdims=True)
        o_ref[...] = _head_unstack(o, tdec)


def _mla_s_kernel(pt_ref, *refs, pp, tdec):
    pages = refs[:pp]
    ql_ref, qr_ref, rc_ref, wuv_ref, o_ref, m_s, l_s, acc_s = refs[pp:]
    s = pl.program_id(1)
    rows = H_C * tdec
    qlat, qrope = _mla_q_rows(ql_ref[...], qr_ref[...])
    scale = (D_NOPE + D_ROPE) ** -0.5

    @pl.when(s == 0)
    def _():
        m_s[...] = jnp.full(m_s.shape, -jnp.inf, F32)
        l_s[...] = jnp.zeros_like(l_s)
        acc_s[...] = jnp.zeros_like(acc_s)

    ckvt = _cat_pages(pages, 0, KV_LORA).astype(BF16)
    krt = _cat_pages(pages, KV_LORA, ROW_C).astype(BF16)
    sc = (jnp.dot(qlat, ckvt, preferred_element_type=F32) + jnp.dot(qrope, krt, preferred_element_type=F32)) * scale
    m_i, l_i, acc = _softmax_step(sc, lambda p: _dot_nt(p, ckvt), m_s[...], l_s[...], acc_s[...])
    m_s[...] = m_i
    l_s[...] = l_i
    acc_s[...] = acc

    @pl.when(s == pl.num_programs(1) - 1)
    def _():
        ckv = _pad_rows(rc_ref[:, 0:KV_LORA], LANES).astype(BF16)
        kr = _pad_rows(rc_ref[:, KV_LORA:ROW_C], LANES).astype(BF16)
        trow = jnp.concatenate([_row((tdec, LANES))] * H_C, axis=0)
        sn = (_dot_nt(qlat, ckv) + _dot_nt(qrope, kr)) * scale
        sn = jnp.where(_lane((rows, LANES)) <= trow, sn, NEG)
        _, l_f, acc_f = _softmax_step(sn, lambda p: _dot(p, ckv), m_s[...], l_s[...], acc_s[...])
        o_ref[...] = _mla_out(acc_f / l_f, wuv_ref, tdec)


def _bias_lookup(tab, bk):
    oh = jax.nn.one_hot(bk, N_BUCKETS, dtype=F32)
    return jnp.einsum('...b,bh->h...', oh, tab.astype(F32), precision=lax.Precision.HIGHEST)


def _bias_tables(tab, n_var, tq, chk, step):
    heads = tab.shape[1]
    i = jnp.arange(tq, dtype=I32)[:, None]
    j = jnp.arange(chk, dtype=I32)[None, :]
    out = []
    for r in range(n_var):
        out.append(_bias_lookup(tab, _t5_bucket(i - j + step * r)).reshape(heads * tq, chk))
    return jnp.stack(out)


def _tail_tables(tab, tdec):
    heads = tab.shape[1]
    t = jnp.arange(tdec, dtype=I32)[:, None]
    j = jnp.arange(2 * LANES, dtype=I32)[None, :]
    full = _bias_lookup(tab, _t5_bucket(t + LANES - j)).reshape(heads * tdec, 2 * LANES)
    far = tab[_t5_bucket(jnp.int32(1 << 20))].astype(F32)
    cb = jnp.repeat(far, tdec)[:, None]
    return cb, full - cb


def _pick_tile(n, cap, mult):
    best = mult
    for c in range(mult, cap + 1, mult):
        if n % c == 0:
            best = c
    return best


def kernel(x_prompt, x_sample, cache_a, cache_b, cache_c, state_d, page_table, p_prompt, p_sample, norm1_g, w_in, cq_norm_g, w_uq, ckv_norm_g, w_uk, w_uv, lb_logits, d_norm_g, t5_bias, w_branch, w_out, norm2_g, peer_wq, peer_subkeys, peer_u, peer_v, ple_gate, ple_proj, final_norm_g):
    bp, tp, d = x_prompt.shape
    bs, ts, _ = x_sample.shape
    depth = w_in.shape[0]
    npages = page_table.shape[1]
    page = cache_a.shape[2]
    past = npages * page
    n_p = bp * tp
    n_s = bs * ts
    n = n_p + n_s
    assert page == LANES and ts == 8 and tp % MOBA_BLOCK == 0 and past % MOBA_BLOCK == 0
    tm = _pick_tile(math.gcd(n_p, n_s), 256, 8)
    assert n_p % tm == 0 and n % LANES == 0
    tn_peer = _pick_tile(n, 640, LANES)
    tq = 128
    chk = min(512, tp)
    pp = 8 if npages % 8 == 0 else 2
    assert tp % chk == 0 and npages % pp == 0

    x = jnp.concatenate([x_prompt.reshape(n_p, d), x_sample.reshape(n_s, d)], axis=0)
    p_all = jnp.concatenate([p_prompt.reshape(depth, n_p, -1), p_sample.reshape(depth, n_s, -1)], axis=1)

    sm = jax.nn.softmax(lb_logits.astype(F32), axis=0)
    lb_all = jnp.maximum(jnp.cumsum(sm, axis=0) - sm[0], 0.0)
    zc = lambda k: jnp.zeros((depth, d, k), F32)
    o = np.cumsum((0,) + (H_A * DH, DH, DH, H_IDX * D_IDX, H_IDX, D_IDX, H_B * DH, G_B * DH, G_B * DH,
                          Q_LORA, KV_LORA, D_ROPE, 4 * H_D * DK_D))
    seg = lambda a, b: w_in[:, :, o[a]:o[b]]
    w_cat = jnp.concatenate([
        seg(12, 13),
        seg(0, 1),
        seg(1, 3), seg(5, 6), seg(4, 5), zc(256 - ROW_A - H_IDX),
        seg(6, 7), seg(7, 9),
        seg(9, 10), zc(256 - Q_LORA),
        seg(3, 4), seg(10, 11), seg(11, 12), zc(LANES - D_ROPE)], axis=2).astype(BF16)
    w_gate = w_in[:, :, o[13]:].astype(BF16)
    cqg = jnp.pad(cq_norm_g, ((0, 0), (0, 256 - Q_LORA)))[:, None, :]
    wuq3 = w_uq.reshape(depth, Q_LORA, H_C, D_NOPE + D_ROPE)
    wuq_p = jnp.concatenate([wuq3[..., :D_NOPE].reshape(depth, Q_LORA, H_C * D_NOPE),
                             wuq3[..., D_NOPE:].reshape(depth, Q_LORA, H_C * D_ROPE)], axis=2)
    wuq_p = jnp.pad(wuq_p, ((0, 0), (0, 256 - Q_LORA), (0, 0))).astype(BF16)
    wukbd = jnp.zeros((depth, H_C, D_NOPE, H_C, KV_LORA), F32)
    wuvp = jnp.zeros((depth, H_C, KV_LORA, H_C, D_V_C), F32)
    for h in range(H_C):
        wukbd = wukbd.at[:, h, :, h, :].set(jnp.swapaxes(w_uk[:, :, h, :], 1, 2))
        wuvp = wuvp.at[:, h, :, h, :].set(w_uv[:, :, h, :])
    wukbd = wukbd.reshape(depth, H_C * D_NOPE, H_C * KV_LORA).astype(BF16)
    wuvp = wuvp.reshape(depth, H_C, KV_LORA, H_C * D_V_C).astype(BF16)
    dgt = jnp.tile(d_norm_g, (1, H_D))[:, None, :]
    hd = lax.broadcasted_iota(I32, (H_D * DK_D, H_D * DK_D), 0) // DK_D
    ones_bd = (hd == hd.T).astype(BF16)
    wb = w_branch.astype(BF16)
    wo = w_out.astype(BF16)
    wq = peer_wq.astype(BF16)
    sk = peer_subkeys.reshape(depth, 2 * PEER_HEADS, N_KEYS, -1).astype(BF16)
    u_b = peer_u.astype(BF16)
    vt_b = jnp.swapaxes(peer_v, 1, 2).astype(BF16)
    pg = ple_gate.astype(BF16)
    ppj = ple_proj.astype(BF16)

    pos = jnp.concatenate([jnp.tile(jnp.arange(tp, dtype=I32), bp),
                           jnp.tile(past + jnp.arange(ts, dtype=I32), bs)])
    half = D_ROPE // 2
    inv = 1.0 / (ROPE_THETA ** (jnp.arange(half, dtype=F32) / half))
    ang = pos.astype(F32)[:, None] * inv
    cos_t = jnp.tile(jnp.cos(ang), (1, LANES // half))
    sgn = jnp.where((jnp.arange(LANES) % D_ROPE) < half, -1.0, 1.0).astype(F32)
    sin_t = jnp.tile(jnp.sin(ang), (1, LANES // half)) * sgn
    bt_a = _bias_tables(t5_bias[:, :H_A], chk // LANES + 2, tq, chk, LANES)
    bt_b = _bias_tables(t5_bias[:, H_A:], 3, MOBA_BLOCK, MOBA_BLOCK, MOBA_BLOCK)
    cb_a, tail_a = _tail_tables(t5_bias[:, :H_A], ts)
    cb_b, tail_b = _tail_tables(t5_bias[:, H_A:], ts)
    tri_c = (lax.broadcasted_iota(I32, (chk, chk), 0) < lax.broadcasted_iota(I32, (chk, chk), 1)).astype(BF16)
    tri_l = tri_c[:LANES, :LANES]
    topk_p = min(TOPK_A_MAX, tp // 4)
    topk_s = min(TOPK_A_MAX, (past + ts) // 4)
    w = H_D * DK_D
    st0_p = jnp.zeros((bp, w, w), F32)
    lp = past + LANES
    rows_dec = H_A * ts
    npool = cache_a.shape[1]
    cache_at = jnp.swapaxes(cache_a, 2, 3)
    cache_bt = jnp.transpose(cache_b, (0, 1, 3, 4, 5, 2)).reshape(depth, npool, 2 * G_B * DH, page)
    cache_ct = jnp.swapaxes(cache_c, 2, 3)
    nbl = -(-(past // MOBA_BLOCK) // LANES) * LANES

    rows_a, rows_b, rows_c, st_p, st_s = [], [], [], [], []
    y = None
    for li in range(depth):
        p = _proj_in(x, norm1_g[li][None, :], w_cat[li], tm)
        rc, ql, qr = _mla_prep(p, cos_t, sin_t, cqg[li], ckv_norm_g[li][None, :], wuq_p[li], wukbd[li], tm)
        rows_a.append(p[:, COL_AKV:COL_AKV + ROW_A])
        rows_b.append(p[:, COL_BKV:COL_BKV + 2 * G_B * DH])
        rows_c.append(rc)

        oa_p = _dsa_prompt(p, bt_a, tri_c, bp, tp, tq, chk, topk_p)
        ob_p = _moba_prompt(p, bt_b, bp, tp)
        oc_p = _mla_prompt(ql, qr, rc, wuvp[li], bp, tp, tq, chk)
        lbl = lb_all[li][None, :]
        od_p, stp = _hgrn(p, lbl, dgt[li], ones_bd, st0_p, bp, tp, 0, min(tp, 256), math.gcd(tp, CHUNK_D))
        st0_s = jnp.zeros((bs, H_D, DV_D, H_D, DK_D), F32)
        for h in range(H_D):
            st0_s = st0_s.at[:, h, :, h, :].set(jnp.swapaxes(state_d[li, :, h].astype(F32), 1, 2))
        st0_s = st0_s.reshape(bs, w, w)
        od_s, sts = _hgrn(p, lbl, dgt[li], ones_bd, st0_s, bs, ts, n_p, ts, math.gcd(ts, CHUNK_D))

        oa_s = _sample_call(
            functools.partial(_dsa_s_kernel, pp=pp, past=past, topk=topk_s, tdec=ts), "dsa_sample",
            cache_at, li, page_table, pp,
            [(p, 256, COL_AQ // 256), (p, 256, COL_AKV // 256), (p, 128, COL_AIQ // 128)],
            [cb_a, tail_a, tri_l], bs, ts, n_p,
            [pltpu.VMEM((ts, lp), I32), pltpu.VMEM((rows_dec, lp), F32), pltpu.VMEM((DH, past), BF16)])
        ob_s = _sample_call(
            functools.partial(_moba_s_kernel, pp=pp, past=past,
                              ntop=min(MOBA_TOPK, past // MOBA_BLOCK + 1), tdec=ts), "moba_sample",
            cache_bt, li, page_table, pp,
            [(p, 256, COL_BQ // 256), (p, 256, COL_BKV // 256)],
            [cb_b, tail_b], bs, ts, n_p,
            [pltpu.VMEM((rows_dec, lp), F32), pltpu.VMEM((G_B * DH, past), BF16),
             pltpu.VMEM((G_B * DH, nbl), F32)])
        oc_s = _sample_call(
            functools.partial(_mla_s_kernel, pp=pp, tdec=ts), "mla_sample",
            cache_ct, li, page_table, pp,
            [(ql, H_C * KV_LORA, 0), (qr, LANES, 0), (rc, ROW_C, 0)],
            [wuvp[li]], bs, ts, n_p,
            [pltpu.VMEM((rows_dec, 1), F32), pltpu.VMEM((rows_dec, 1), F32), pltpu.VMEM((rows_dec, KV_LORA), F32)])
        st_p.append(stp)
        st_s.append(sts)

        cat = lambda a, b: jnp.concatenate([a, b], axis=0)
        x1 = _merge(x, cat(oa_p, oa_s), cat(ob_p, ob_s), cat(oc_p, oc_s), cat(od_p, od_s),
                    norm1_g[li][None, :], w_gate[li], wb[li], wo[li], tm)
        g2 = norm2_g[li][None, :]
        th, s1, e0, e1 = _peer_select(x1, g2, wq[li], sk[li], _pick_tile(n, 256, LANES))
        po = _peer_dense(x1, g2, u_b[li], vt_b[li], th, s1, e0, e1, tn_peer, 8)
        x, y = _ple(x1, po, p_all[li], pg[li], ppj[li], final_norm_g[None, :], tm)

    def unstate(st, nb):
        s5 = jnp.stack(st).reshape(depth, nb, H_D, DV_D, H_D, DK_D)
        diag = jnp.stack([s5[:, :, h, :, h, :] for h in range(H_D)], axis=2)
        return jnp.swapaxes(diag, 3, 4)

    ra = jnp.stack(rows_a)
    rb = jnp.stack(rows_b)
    rcs = jnp.stack(rows_c)
    return (y[:n_p].reshape(bp, tp, d), y[n_p:].reshape(bs, ts, d),
            ra[:, :n_p].reshape(depth, bp, tp, ROW_A),
            rb[:, :n_p].reshape(depth, bp, tp, 2, G_B, DH),
            rcs[:, :n_p].reshape(depth, bp, tp, ROW_C),
            unstate(st_p, bp),
            ra[:, n_p:].reshape(depth, bs, ts, ROW_A),
            rb[:, n_p:].reshape(depth, bs, ts, 2, G_B, DH),
            rcs[:, n_p:].reshape(depth, bs, ts, ROW_C),
            unstate(st_s, bs))
```

```python
import functools
import math

import jax
import jax.numpy as jnp
import numpy as np
from jax import lax
from jax.experimental import pallas as pl
from jax.experimental.pallas import tpu as pltpu

F32 = jnp.float32
BF16 = jnp.bfloat16
I32 = jnp.int32

DH = 64
H_A = 4
H_IDX = 4
D_IDX = 32
TOPK_A_MAX = 256
H_B = 4
G_B = 2
MOBA_BLOCK = 256
MOBA_TOPK = 3
H_C = 4
Q_LORA = 192
KV_LORA = 128
D_NOPE = 64
D_ROPE = 32
D_V_C = 64
ROPE_THETA = 10000.0
H_D = 4
DK_D = 64
DV_D = 64
CHUNK_D = 16
N_BRANCH = 4
BRANCH_W = 256
N_BUCKETS = 32
MAX_DIST = 128
N_KEYS = 128
PEER_HEADS = 8
PEER_TOPK = 16
EPS = 1e-6
NEG = -1e30
F_MIN = 1e-20
ROW_A = 2 * DH + D_IDX
ROW_C = KV_LORA + D_ROPE

LANES = 128
VMEM_LIMIT = 56 * 1024 * 1024
INT_MIN = -(2 ** 31)

COL_D = 0
COL_AQ = 1024
COL_AKV = 1280
COL_BQ = 1536
COL_BKV = 1792
COL_CQ = 2048
COL_AIQ = 2304
COL_CKV = 2432
COL_CKR = 2560
P_COLS = 2688


def _cparams(*sem):
    return pltpu.CompilerParams(dimension_semantics=sem, vmem_limit_bytes=VMEM_LIMIT)


def _dot(a, b):
    return jnp.dot(a.astype(BF16), b.astype(BF16), preferred_element_type=F32)


def _dot_nt(a, b):
    return lax.dot_general(a.astype(BF16), b.astype(BF16), (((1,), (1,)), ((), ())),
                           preferred_element_type=F32)


def _split(a):
    hi = a.astype(BF16)
    lo = (a - hi.astype(F32)).astype(BF16)
    return hi, lo


def _dot3_nt(a, b):
    ah, al = _split(a)
    bh, bl = _split(b)
    dn = (((1,), (1,)), ((), ()))
    d = lambda x, y: lax.dot_general(x, y, dn, preferred_element_type=F32)
    return d(ah, bh) + (d(ah, bl) + d(al, bh))


def _rms(x, g):
    return x * lax.rsqrt(jnp.mean(x * x, axis=-1, keepdims=True) + EPS) * g


def _t5_bucket(rel):
    n = jnp.maximum(rel, 0)
    exact = N_BUCKETS // 2
    nf = jnp.maximum(n, exact).astype(F32)
    big = exact + (jnp.log(nf / exact) / math.log(MAX_DIST / exact) * (N_BUCKETS - exact)).astype(I32)
    return jnp.where(n < exact, n, jnp.minimum(big, N_BUCKETS - 1))


def _sort_key(s):
    bits = pltpu.bitcast(s, I32)
    return jnp.where(bits < 0, bits ^ jnp.int32(0x7FFFFFFF), bits)


def _lane(shape):
    return lax.broadcasted_iota(I32, shape, 1)


def _row(shape):
    return lax.broadcasted_iota(I32, shape, 0)


def _in_kernel(x_ref, g_ref, w_ref, o_ref):
    h = _rms(x_ref[...], g_ref[...])
    o_ref[...] = _dot(h, w_ref[...])


def _proj_in(x, g, w, tm):
    n, d = x.shape
    return pl.pallas_call(
        _in_kernel,
        grid=(n // tm,),
        in_specs=[pl.BlockSpec((tm, d), lambda i: (i, 0)),
                  pl.BlockSpec((1, d), lambda i: (0, 0)),
                  pl.BlockSpec((d, P_COLS), lambda i: (0, 0))],
        out_specs=pl.BlockSpec((tm, P_COLS), lambda i: (i, 0)),
        out_shape=jax.ShapeDtypeStruct((n, P_COLS), F32),
        compiler_params=_cparams("parallel"),
        name="proj_in",
    )(x, g, w)


def _rope_lanes(x, cos, sin_s):
    lane = _lane(x.shape)
    sw = jnp.where((lane % D_ROPE) < D_ROPE // 2, pltpu.roll(x, LANES - D_ROPE // 2, 1),
                   pltpu.roll(x, D_ROPE // 2, 1))
    return x * cos + sw * sin_s


def _mla_prep_kernel(cq_ref, ckv_ref, ckr_ref, cos_ref, sin_ref, cqg_ref, ckvg_ref, wuq_ref, wuk_ref,
                     rc_ref, ql_ref, qr_ref):
    cq = cq_ref[...]
    ms = jnp.sum(cq * cq, axis=-1, keepdims=True) * (1.0 / Q_LORA)
    hq = cq * lax.rsqrt(ms + EPS) * cqg_ref[...]
    qc = _dot(hq, wuq_ref[...])
    cos = cos_ref[...]
    sin_s = sin_ref[...]
    qr_ref[...] = _rope_lanes(qc[:, H_C * D_NOPE:], cos, sin_s)
    ql_ref[...] = _dot(qc[:, :H_C * D_NOPE], wuk_ref[...])
    rc_ref[:, 0:KV_LORA] = _rms(ckv_ref[...], ckvg_ref[...])
    rc_ref[:, KV_LORA:ROW_C] = _rope_lanes(ckr_ref[...], cos, sin_s)[:, 0:D_ROPE]


def _mla_prep(p, cos_t, sin_t, cqg, ckvg, wuq, wukbd, tm):
    n = p.shape[0]
    blk = lambda w, c: pl.BlockSpec((tm, w), lambda i, c=c: (i, c))
    cst = lambda a: pl.BlockSpec(a.shape, lambda i: (0,) * a.ndim)
    return pl.pallas_call(
        _mla_prep_kernel,
        grid=(n // tm,),
        in_specs=[blk(256, COL_CQ // 256), blk(128, COL_CKV // 128), blk(128, COL_CKR // 128),
                  blk(128, 0), blk(128, 0), cst(cqg), cst(ckvg), cst(wuq), cst(wukbd)],
        out_specs=[pl.BlockSpec((tm, ROW_C), lambda i: (i, 0)),
                   pl.BlockSpec((tm, H_C * KV_LORA), lambda i: (i, 0)),
                   pl.BlockSpec((tm, LANES), lambda i: (i, 0))],
        out_shape=[jax.ShapeDtypeStruct((n, ROW_C), F32),
                   jax.ShapeDtypeStruct((n, H_C * KV_LORA), F32),
                   jax.ShapeDtypeStruct((n, LANES), F32)],
        compiler_params=_cparams("parallel"),
        name="mla_prep",
    )(p, p, p, cos_t, sin_t, cqg, ckvg, wuq, wukbd)


def _merge_kernel(x_ref, oa_ref, ob_ref, oc_ref, od_ref, g_ref, wg_ref, wb_ref, wo_ref, o_ref):
    x = x_ref[...]
    h = _rms(x, g_ref[...]).astype(BF16)
    d = x.shape[1]
    mix = None
    for m, o_m in enumerate((oa_ref, ob_ref, oc_ref, od_ref)):
        gate = jax.nn.sigmoid(jnp.dot(h, wg_ref[:, m * d:(m + 1) * d], preferred_element_type=F32))
        term = gate * _dot(o_m[...], wb_ref[m])
        mix = term if mix is None else mix + term
    o_ref[...] = x + _dot(mix, wo_ref[...])


def _merge(x, oa, ob, oc, od, g, wg, wb, wo, tm):
    n, d = x.shape
    row = lambda w: pl.BlockSpec((tm, w), lambda i: (i, 0))
    cst = lambda a: pl.BlockSpec(a.shape, lambda i: (0,) * a.ndim)
    return pl.pallas_call(
        _merge_kernel,
        grid=(n // tm,),
        in_specs=[row(d), row(BRANCH_W), row(BRANCH_W), row(BRANCH_W), row(BRANCH_W),
                  cst(g), cst(wg), cst(wb), cst(wo)],
        out_specs=row(d),
        out_shape=jax.ShapeDtypeStruct((n, d), F32),
        compiler_params=_cparams("parallel"),
        name="merge",
    )(x, oa, ob, oc, od, g, wg, wb, wo)


def _ple_kernel(x_ref, po_ref, p_ref, pg_ref, pp_ref, gf_ref, o_ref, y_ref):
    x2 = x_ref[...] + po_ref[...]
    xn = x2 + jax.nn.sigmoid(_dot(x2, pg_ref[...])) * _dot(p_ref[...], pp_ref[...])
    o_ref[...] = xn
    y_ref[...] = _rms(xn, gf_ref[...])


def _ple(x, po, p, pg, pp, gf, tm):
    n, d = x.shape
    row = lambda w: pl.BlockSpec((tm, w), lambda i: (i, 0))
    cst = lambda a: pl.BlockSpec(a.shape, lambda i: (0,) * a.ndim)
    return pl.pallas_call(
        _ple_kernel,
        grid=(n // tm,),
        in_specs=[row(d), row(d), row(p.shape[1]), cst(pg), cst(pp), cst(gf)],
        out_specs=[row(d), row(d)],
        out_shape=[jax.ShapeDtypeStruct((n, d), F32), jax.ShapeDtypeStruct((n, d), F32)],
        compiler_params=_cparams("parallel"),
        name="ple",
    )(x, po, p, pg, pp, gf)


PEER_NCAND = 80
PEER_NTOP = 24


def _extract_top(w_s, out_s, n_chains, count):
    def body(r, carry):
        for c in range(n_chains):
            w = w_s[c]
            m = jnp.max(w, axis=0, keepdims=True)
            out_s[c, pl.ds(r, 1), :] = m
            w_s[c] = jnp.where(w == m, -jnp.inf, w)
        return carry
    lax.fori_loop(0, count, body, 0)


def _peer_sel_kernel(x_ref, g_ref, wq_ref, sk_ref, th_ref, s1_ref, e0_ref, e1_ref, w_s, sv_s, c_s, top_s):
    hn = _rms(x_ref[...], g_ref[...])
    q = _dot(hn, wq_ref[...])
    for hp in range(2 * PEER_HEADS):
        s = _dot_nt(sk_ref[hp], q[:, hp * N_KEYS:(hp + 1) * N_KEYS])
        w_s[hp] = s
        if hp % 2 == 0:
            th_ref[hp // 2] = s
        else:
            s1_ref[hp // 2] = s
    _extract_top(w_s, sv_s, 2 * PEER_HEADS, PEER_TOPK)
    half = PEER_TOPK // 2
    for h in range(PEER_HEADS):
        sv0 = sv_s[2 * h]
        sv1 = sv_s[2 * h + 1]
        pieces = [sv0[0:1, :] + sv1]
        pieces += [sv0[a:a + 1, :] + sv1[0:half, :] for a in range(1, half)]
        pieces.append(sv0[half:PEER_TOPK, :] + sv1[0:1, :])
        c_s[h] = jnp.concatenate(pieces, axis=0)
    _extract_top(c_s, top_s, PEER_HEADS, PEER_TOPK + 1)
    for h in range(PEER_HEADS):
        top = top_s[h]
        z = jnp.sum(jnp.exp(top[0:PEER_TOPK, :] - top[0:1, :]), axis=0, keepdims=True)
        cut = 0.5 * (top[PEER_TOPK - 1:PEER_TOPK, :] + top[PEER_TOPK:PEER_TOPK + 1, :])
        s0 = th_ref[h]
        e0_ref[h] = jnp.exp(s0 - sv_s[2 * h, 0:1, :]) / z
        e1_ref[h] = jnp.exp(s1_ref[h] - sv_s[2 * h + 1, 0:1, :])
        th_ref[h] = cut - s0


def _peer_select(x, g, wq, sk, tn):
    n, d = x.shape
    hk = jax.ShapeDtypeStruct((PEER_HEADS, N_KEYS, n), F32)
    hk_spec = pl.BlockSpec((PEER_HEADS, N_KEYS, tn), lambda i: (0, 0, i))
    cst = lambda a: pl.BlockSpec(a.shape, lambda i: (0,) * a.ndim)
    return pl.pallas_call(
        _peer_sel_kernel,
        grid=(n // tn,),
        in_specs=[pl.BlockSpec((tn, d), lambda i: (i, 0)), cst(g), cst(wq), cst(sk)],
        out_specs=[hk_spec, hk_spec, hk_spec, hk_spec],
        out_shape=[hk, hk, hk, hk],
        scratch_shapes=[pltpu.VMEM((2 * PEER_HEADS, N_KEYS, tn), F32),
                        pltpu.VMEM((2 * PEER_HEADS, PEER_TOPK, tn), F32),
                        pltpu.VMEM((PEER_HEADS, PEER_NCAND, tn), F32),
                        pltpu.VMEM((PEER_HEADS, PEER_NTOP, tn), F32)],
        compiler_params=_cparams("parallel"),
        name="peer_select",
    )(x, g, wq, sk)


def _peer_dense_kernel(x_ref, g_ref, u_ref, vt_ref, th_ref, s1_ref, e0_ref, e1_ref, o_ref, hn_s, acc_s,
                       act_s, z_s, *, ic):
    c = pl.program_id(1)
    tn = hn_s.shape[1]

    @pl.when(c == 0)
    def _():
        hn_s[...] = _rms(x_ref[...], g_ref[...]).T.astype(BF16)
        acc_s[...] = jnp.zeros_like(acc_s)
        act_s[...] = jnp.zeros_like(act_s)
        z_s[...] = jnp.zeros_like(z_s)

    cur = c % 2
    prv = 1 - cur
    for ii in range(ic):
        if ii == 1:
            act_s[cur] = jnp.dot(u_ref[...], hn_s[...], preferred_element_type=F32)
        if ii == ic // 2:
            acc_s[...] += jnp.dot(vt_ref[...], z_s[cur], preferred_element_type=F32)
        er = slice(ii * N_KEYS, (ii + 1) * N_KEYS)
        for lt in range(tn // LANES):
            tok = slice(lt * LANES, (lt + 1) * LANES)
            wt = None
            for h in range(PEER_HEADS):
                t = jnp.where(s1_ref[h, :, tok] >= th_ref[h, ii:ii + 1, tok], e1_ref[h, :, tok], 0.0)
                term = e0_ref[h, ii:ii + 1, tok] * t
                wt = term if wt is None else wt + term
            z_s[prv, er, tok] = (wt * jax.nn.gelu(act_s[prv, er, tok])).astype(BF16)

    @pl.when(c == pl.num_programs(1) - 1)
    def _():
        o_ref[...] = acc_s[...].T


def _peer_dense(x, g, u, vt, th, s1, e0, e1, tn, ic):
    n, d = x.shape
    ne = u.shape[0]
    ec = ic * N_KEYS
    nc = ne // ec
    clamp = lambda c: jnp.minimum(jnp.maximum(c, 0), nc - 1)
    return pl.pallas_call(
        functools.partial(_peer_dense_kernel, ic=ic),
        grid=(n // tn, nc + 2),
        in_specs=[pl.BlockSpec((tn, d), lambda i, c: (i, 0)),
                  pl.BlockSpec((1, d), lambda i, c: (0, 0)),
                  pl.BlockSpec((ec, d), lambda i, c: (clamp(c), 0)),
                  pl.BlockSpec((d, ec), lambda i, c: (0, clamp(c - 2))),
                  pl.BlockSpec((PEER_HEADS, ic, tn), lambda i, c: (0, clamp(c - 1), i)),
                  pl.BlockSpec((PEER_HEADS, N_KEYS, tn), lambda i, c: (0, 0, i)),
                  pl.BlockSpec((PEER_HEADS, ic, tn), lambda i, c: (0, clamp(c - 1), i)),
                  pl.BlockSpec((PEER_HEADS, N_KEYS, tn), lambda i, c: (0, 0, i))],
        out_specs=pl.BlockSpec((tn, d), lambda i, c: (i, 0)),
        out_shape=jax.ShapeDtypeStruct((n, d), F32),
        scratch_shapes=[pltpu.VMEM((d, tn), BF16), pltpu.VMEM((d, tn), F32),
                        pltpu.VMEM((2, ec, tn), F32), pltpu.VMEM((2, ec, tn), BF16)],
        compiler_params=_cparams("parallel", "arbitrary"),
        name="peer_dense",
    )(x, g, u, vt, th, s1, e0, e1)


def _hgrn_kernel(p_ref, lb_ref, dg_ref, ones_ref, st0_ref, o_ref, st_ref, q_s, k_s, b_s, v_s, o_s,
                 *, ts, ch):
    s = pl.program_id(1)
    w = H_D * DK_D

    @pl.when(s == 0)
    def _():
        st_ref[...] = st0_ref[...]

    dq = p_ref[:, 0:w]
    df = p_ref[:, w:2 * w]
    lb = lb_ref[...]
    sig = jax.nn.sigmoid(df)
    logf = jnp.log(jnp.maximum(lb + (1.0 - lb) * sig, F_MIN))
    r = _row((ts, w)) % ch
    b = logf
    sh = 1
    while sh < ch:
        b = b + jnp.where(r >= sh, pltpu.roll(b, sh, 0), 0.0)
        sh *= 2
    q_s[...] = dq * jax.nn.sigmoid(dq)
    k_s[...] = (1.0 - lb) * (1.0 - sig)
    b_s[...] = b
    v_s[...] = p_ref[:, 2 * w:3 * w]
    ones_bd = ones_ref[...]
    bd_mask = ones_bd.astype(F32)
    srow = _row((ch, w))

    def body(c, carry):
        off = pl.multiple_of(c * ch, ch)
        q = q_s[pl.ds(off, ch), :]
        kk = k_s[pl.ds(off, ch), :]
        bb = b_s[pl.ds(off, ch), :]
        v = v_s[pl.ds(off, ch), :]
        st = st_ref[0]
        o_inter = _dot_nt(q * jnp.exp(bb), st)
        rows = []
        for t in range(ch):
            dec = jnp.exp(jnp.where(srow <= t, bb[t:t + 1, :] - bb, 0.0))
            rows.append(jnp.where(srow <= t, dec * kk * q[t:t + 1, :], 0.0))
        pm = jnp.concatenate(rows, axis=0)
        rr = jnp.dot(pm.astype(BF16), ones_bd, preferred_element_type=F32)
        o_intra = jnp.sum(rr.reshape(ch, ch, w) * v[None, :, :], axis=1)
        o_s[pl.ds(off, ch), :] = o_inter + o_intra
        bl = bb[ch - 1:ch, :]
        kdec = kk * jnp.exp(bl - bb)
        upd = lax.dot_general(v.astype(BF16), kdec.astype(BF16), (((0,), (0,)), ((), ())),
                              preferred_element_type=F32)
        st_ref[0] = (st * jnp.exp(bl) + upd) * bd_mask
        return carry

    lax.fori_loop(0, ts // ch, body, 0)
    o = o_s[...]
    ms = jnp.dot((o * o).astype(BF16), ones_bd, preferred_element_type=F32) * (1.0 / DV_D)
    dg = p_ref[:, 3 * w:4 * w]
    o_ref[...] = o * lax.rsqrt(ms + EPS) * dg_ref[...] * (dg * jax.nn.sigmoid(dg))


def _hgrn(p, lb, dg, ones_bd, st0, nb, t, row0, ts, ch):
    w = H_D * DK_D
    ns = t // ts
    rb0 = row0 // ts
    cst = lambda a: pl.BlockSpec(a.shape, lambda b, s: (0,) * a.ndim)
    return pl.pallas_call(
        functools.partial(_hgrn_kernel, ts=ts, ch=ch),
        grid=(nb, ns),
        in_specs=[pl.BlockSpec((ts, 4 * w), lambda b, s: (rb0 + b * ns + s, COL_D // (4 * w))),
                  cst(lb), cst(dg), cst(ones_bd),
                  pl.BlockSpec((1, w, w), lambda b, s: (b, 0, 0))],
        out_specs=[pl.BlockSpec((ts, w), lambda b, s: (b * ns + s, 0)),
                   pl.BlockSpec((1, w, w), lambda b, s: (b, 0, 0))],
        out_shape=[jax.ShapeDtypeStruct((nb * t, w), F32), jax.ShapeDtypeStruct((nb, w, w), F32)],
        scratch_shapes=[pltpu.VMEM((ts, w), F32)] * 5,
        compiler_params=_cparams("parallel", "arbitrary"),
        name="hgrn",
    )(p, lb, dg, ones_bd, st0)


def _radix_threshold(count_ge, k, rows):
    def body(i, t):
        cand = t ^ jnp.left_shift(jnp.int32(1), 31 - i)
        return jnp.where(count_ge(cand) >= k, cand, t)
    return lax.fori_loop(0, 32, body, jnp.full((rows, 1), INT_MIN, I32))


def _head_rows_low(q, scale):
    lane = _lane((q.shape[0], LANES))
    out = []
    for m in range(2):
        tile = q[:, m * LANES:(m + 1) * LANES] * scale
        out.append(jnp.where(lane < DH, tile, 0.0))
        out.append(jnp.where(lane < DH, pltpu.roll(tile, DH, 1), 0.0))
    return jnp.concatenate(out, axis=0)


def _head_rows_group(q, scale):
    lane = _lane((q.shape[0], LANES))
    t0 = q[:, 0:LANES] * scale
    t1 = q[:, LANES:2 * LANES] * scale
    return jnp.concatenate([
        jnp.where(lane < DH, t0, 0.0),
        jnp.where(lane < DH, pltpu.roll(t0, DH, 1), 0.0),
        jnp.where(lane >= DH, pltpu.roll(t1, DH, 1), 0.0),
        jnp.where(lane >= DH, t1, 0.0)], axis=0)


def _idx_heads(iq):
    lane = _lane(iq.shape)
    return [jnp.where(lane < D_IDX, iq if h == 0 else pltpu.roll(iq, LANES - D_IDX * h, 1), 0.0)
            for h in range(H_IDX)]


def _index_score(iqh, iw, ik_tile):
    s = None
    for h in range(H_IDX):
        r = jnp.maximum(_dot3_nt(iqh[h], ik_tile), 0.0)
        term = iw[:, h:h + 1] * r
        s = term if s is None else s + term
    return s * ((H_IDX * D_IDX) ** -0.5) + 0.0


def _softmax_step(lg, pv, m_i, l_i, acc):
    m_new = jnp.maximum(m_i, jnp.max(lg, axis=1, keepdims=True))
    alpha = jnp.exp(m_i - m_new)
    p = jnp.exp(lg - m_new)
    l_new = alpha * l_i + jnp.sum(p, axis=1, keepdims=True)
    return m_new, l_new, alpha * acc + pv(p)


def _dsa_p_kernel(q_ref, qkv_ref, iq_ref, kv_ref, bt_ref, tri_ref, o_ref, key_s, *, tq, chk, topk):
    qi = pl.program_id(1)
    nch = (qi * tq + tq + chk - 1) // chk
    qpos = qi * tq + _row((tq, 1))
    iw = qkv_ref[:, ROW_A:ROW_A + H_IDX]
    iqh = _idx_heads(iq_ref[...])
    nv = bt_ref.shape[0]

    def kpos_of(off):
        return off + _lane((tq, chk))

    def score_body(c, carry):
        off = pl.multiple_of(c * chk, chk)
        s = _index_score(iqh, iw, kv_ref[pl.ds(off, chk), LANES:2 * LANES])
        s = jnp.where(kpos_of(off) <= qpos, s, NEG)
        key_s[:, pl.ds(off, chk)] = _sort_key(s)
        return carry

    lax.fori_loop(0, nch, score_body, 0)

    def count(pred):
        def body(c, acc):
            off = pl.multiple_of(c * chk, chk)
            m = pred(key_s[:, pl.ds(off, chk)]).astype(I32)
            for l in range(chk // LANES):
                acc = acc + m[:, l * LANES:(l + 1) * LANES]
            return acc
        acc = lax.fori_loop(0, nch, body, jnp.zeros((tq, LANES), I32))
        return jnp.sum(acc, axis=1, keepdims=True)

    thr = _radix_threshold(lambda cand: count(lambda kc: kc >= cand), topk, tq)
    need = (topk - count(lambda kc: kc > thr)).astype(F32)

    qs = _head_rows_low(q_ref[...], DH ** -0.5).astype(BF16)

    def att_body(c, carry):
        m_i, l_i, acc, run = carry
        off = pl.multiple_of(c * chk, chk)
        kc = key_s[:, pl.ds(off, chk)]
        eq = kc == thr
        eqf = jnp.where(eq, 1.0, 0.0)
        pref = jnp.dot(eqf.astype(BF16), tri_ref[...], preferred_element_type=F32) + run
        sel = ((kc > thr) | (eq & (pref < need))) & (kpos_of(off) <= qpos)
        kvt = kv_ref[pl.ds(off, chk), 0:LANES].astype(BF16)
        lg = lax.dot_general(qs, kvt, (((1,), (1,)), ((), ())), preferred_element_type=F32)
        r = jnp.minimum(qi - c * (chk // tq), nv - 1)
        lg = lg + bt_ref[r]
        lg = lg + jnp.concatenate([jnp.where(sel, 0.0, NEG)] * H_A, axis=0)
        m_i, l_i, acc = _softmax_step(lg, lambda p: _dot(p, kvt), m_i, l_i, acc)
        return m_i, l_i, acc, run + jnp.sum(eqf, axis=1, keepdims=True)

    init = (jnp.full((H_A * tq, 1), -jnp.inf, F32), jnp.zeros((H_A * tq, 1), F32),
            jnp.zeros((H_A * tq, LANES), F32), jnp.zeros((tq, 1), F32))
    _, l_i, acc, _ = lax.fori_loop(0, nch, att_body, init)
    o = acc / l_i
    lane = _lane((tq, LANES))
    for m in range(2):
        o_ref[:, m * LANES:(m + 1) * LANES] = jnp.where(
            lane < DH, pltpu.roll(o[2 * m * tq:(2 * m + 1) * tq], DH, 1), o[(2 * m + 1) * tq:(2 * m + 2) * tq])


def _dsa_prompt(p, bt, tri, nb, t, tq, chk, topk):
    nq = t // tq
    return pl.pallas_call(
        functools.partial(_dsa_p_kernel, tq=tq, chk=chk, topk=topk),
        grid=(nb, nq),
        in_specs=[pl.BlockSpec((tq, 256), lambda b, i: (b * nq + i, COL_AQ // 256)),
                  pl.BlockSpec((tq, 256), lambda b, i: (b * nq + i, COL_AKV // 256)),
                  pl.BlockSpec((tq, 128), lambda b, i: (b * nq + i, COL_AIQ // 128)),
                  pl.BlockSpec((t, 256), lambda b, i: (b, COL_AKV // 256)),
                  pl.BlockSpec(bt.shape, lambda b, i: (0, 0, 0)),
                  pl.BlockSpec(tri.shape, lambda b, i: (0, 0))],
        out_specs=pl.BlockSpec((tq, BRANCH_W), lambda b, i: (b * nq + i, 0)),
        out_shape=jax.ShapeDtypeStruct((nb * t, BRANCH_W), F32),
        scratch_shapes=[pltpu.VMEM((tq, t), I32)],
        compiler_params=_cparams("parallel", "arbitrary"),
        name="dsa_prompt",
    )(p, p, p, p, bt, tri)


def _moba_p_kernel(q_ref, kv_ref, bt_ref, o_ref, m_s, l_s, acc_s, *, nblk, ntop):
    qi = pl.program_id(1)
    tq = MOBA_BLOCK
    rows = H_B * tq
    qs = _head_rows_group(q_ref[...], DH ** -0.5)
    kmean = jnp.sum(kv_ref[:, 0:LANES].reshape(nblk, MOBA_BLOCK, LANES), axis=1) * (1.0 / MOBA_BLOCK)
    gate = _dot3_nt(qs, kmean)
    n_l = _lane((rows, nblk))
    gate = jnp.where(n_l < qi, gate, NEG)
    rank = jnp.zeros((rows, nblk), I32)
    for m in range(nblk):
        col = gate[:, m:m + 1]
        beats = (col > gate) | ((col == gate) & (m < n_l))
        rank = rank + beats.astype(I32)
    bm = ((rank < ntop) & (gate > 0.5 * NEG)) | (n_l == qi)
    bmf = jnp.where(bm, 1.0, 0.0)
    m_s[...] = jnp.full(m_s.shape, -jnp.inf, F32)
    l_s[...] = jnp.zeros(l_s.shape, F32)
    acc_s[...] = jnp.zeros(acc_s.shape, F32)
    qb = qs.astype(BF16)
    qpos = qi * tq + _row((tq, 1))
    qpos4 = jnp.concatenate([qpos] * H_B, axis=0)
    for n in range(nblk):
        @pl.when(n <= qi)
        def _(n=n):
            kt = kv_ref[n * MOBA_BLOCK:(n + 1) * MOBA_BLOCK, 0:LANES].astype(BF16)
            vt = kv_ref[n * MOBA_BLOCK:(n + 1) * MOBA_BLOCK, LANES:2 * LANES].astype(BF16)
            lg = lax.dot_general(qb, kt, (((1,), (1,)), ((), ())), preferred_element_type=F32)
            lg = lg + bt_ref[jnp.minimum(qi - n, bt_ref.shape[0] - 1)]
            kpos = n * MOBA_BLOCK + _lane((rows, MOBA_BLOCK))
            ok = (bmf[:, n:n + 1] > 0.5) & (kpos <= qpos4)
            lg = jnp.where(ok, lg, NEG)
            m_i, l_i, acc = _softmax_step(lg, lambda p: _dot(p, vt), m_s[...], l_s[...], acc_s[...])
            m_s[...] = m_i
            l_s[...] = l_i
            acc_s[...] = acc
    o = acc_s[...] / l_s[...]
    lane = _lane((tq, LANES))
    o_ref[:, 0:LANES] = jnp.where(lane < DH, o[0:tq], pltpu.roll(o[tq:2 * tq], DH, 1))
    o_ref[:, LANES:2 * LANES] = jnp.where(lane < DH, pltpu.roll(o[2 * tq:3 * tq], DH, 1), o[3 * tq:4 * tq])


def _moba_prompt(p, bt, nb, t):
    tq = MOBA_BLOCK
    nq = t // tq
    rows = H_B * tq
    return pl.pallas_call(
        functools.partial(_moba_p_kernel, nblk=nq, ntop=min(MOBA_TOPK, nq)),
        grid=(nb, nq),
        in_specs=[pl.BlockSpec((tq, 256), lambda b, i: (b * nq + i, COL_BQ // 256)),
                  pl.BlockSpec((t, 256), lambda b, i: (b, COL_BKV // 256)),
                  pl.BlockSpec(bt.shape, lambda b, i: (0, 0, 0))],
        out_specs=pl.BlockSpec((tq, BRANCH_W), lambda b, i: (b * nq + i, 0)),
        out_shape=jax.ShapeDtypeStruct((nb * t, BRANCH_W), F32),
        scratch_shapes=[pltpu.VMEM((rows, 1), F32), pltpu.VMEM((rows, 1), F32), pltpu.VMEM((rows, LANES), F32)],
        compiler_params=_cparams("parallel", "arbitrary"),
        name="moba_prompt",
    )(p, p, bt)


def _mla_q_rows(ql, qr):
    qlat = jnp.concatenate([ql[:, h * KV_LORA:(h + 1) * KV_LORA] for h in range(H_C)], axis=0)
    qrope = jnp.concatenate([qr[:, h * D_ROPE:(h + 1) * D_ROPE] for h in range(H_C)], axis=0)
    return qlat.astype(BF16), qrope.astype(BF16)


def _mla_out(o_lat, wuv_ref, t):
    out = None
    for h in range(H_C):
        term = _dot(o_lat[h * t:(h + 1) * t], wuv_ref[h])
        out = term if out is None else out + term
    return out


def _mla_p_kernel(ql_ref, qr_ref, rc_ref, wuv_ref, o_ref, *, tq, chk):
    qi = pl.program_id(1)
    nch = (qi * tq + tq + chk - 1) // chk
    rows = H_C * tq
    qlat, qrope = _mla_q_rows(ql_ref[...], qr_ref[...])
    qpos = qi * tq + _row((tq, 1))
    qpos4 = jnp.concatenate([qpos] * H_C, axis=0)
    scale = (D_NOPE + D_ROPE) ** -0.5
    dn = (((1,), (1,)), ((), ()))

    def body(c, carry):
        m_i, l_i, acc = carry
        off = pl.multiple_of(c * chk, chk)
        ckv = rc_ref[pl.ds(off, chk), 0:KV_LORA].astype(BF16)
        kr = rc_ref[pl.ds(off, chk), KV_LORA:ROW_C].astype(BF16)
        s = (lax.dot_general(qlat, ckv, dn, preferred_element_type=F32)
             + lax.dot_general(qrope, kr, dn, preferred_element_type=F32)) * scale
        kpos = off + _lane((rows, chk))
        s = jnp.where(kpos <= qpos4, s, NEG)
        return _softmax_step(s, lambda p: _dot(p, ckv), m_i, l_i, acc)

    init = (jnp.full((rows, 1), -jnp.inf, F32), jnp.zeros((rows, 1), F32), jnp.zeros((rows, KV_LORA), F32))
    _, l_i, acc = lax.fori_loop(0, nch, body, init)
    o_ref[...] = _mla_out(acc / l_i, wuv_ref, tq)


def _mla_prompt(ql, qr, rc, wuvp, nb, t, tq, chk):
    nq = t // tq
    return pl.pallas_call(
        functools.partial(_mla_p_kernel, tq=tq, chk=chk),
        grid=(nb, nq),
        in_specs=[pl.BlockSpec((tq, H_C * KV_LORA), lambda b, i: (b * nq + i, 0)),
                  pl.BlockSpec((tq, LANES), lambda b, i: (b * nq + i, 0)),
                  pl.BlockSpec((t, ROW_C), lambda b, i: (b, 0)),
                  pl.BlockSpec(wuvp.shape, lambda b, i: (0, 0, 0))],
        out_specs=pl.BlockSpec((tq, BRANCH_W), lambda b, i: (b * nq + i, 0)),
        out_shape=jax.ShapeDtypeStruct((nb * t, BRANCH_W), F32),
        compiler_params=_cparams("parallel", "arbitrary"),
        name="mla_prompt",
    )(ql, qr, rc, wuvp)


def _page_specs(shape_tail, li, pp):
    nd = len(shape_tail)
    return [pl.BlockSpec((None, None) + shape_tail,
                         lambda b, s, pt, j=j: (li, pt[b, s * pp + j]) + (0,) * nd)
            for j in range(pp)]


def _cat_pages(pages, lo, hi):
    return jnp.concatenate([pg[lo:hi, :] for pg in pages], axis=1)


def _pad_rows(x, rows):
    return jnp.concatenate([x, jnp.zeros((rows - x.shape[0], x.shape[1]), x.dtype)], axis=0)


def _head_stack(x, width):
    return jnp.concatenate([x[:, h * width:(h + 1) * width] for h in range(x.shape[1] // width)], axis=0)


def _head_unstack(o, t):
    return jnp.concatenate([o[h * t:(h + 1) * t] for h in range(o.shape[0] // t)], axis=1)


def _dot3(a, b):
    ah, al = _split(a)
    bh, bl = _split(b)
    d = lambda x, y: jnp.dot(x, y, preferred_element_type=F32)
    return d(ah, bh) + (d(ah, bl) + d(al, bh))


def _tail_bias(lg, cb_ref, tail_ref):
    lg = lg + cb_ref[...]
    w = lg.shape[1] - 2 * LANES
    return jnp.concatenate([lg[:, :w], lg[:, w:] + tail_ref[...]], axis=1)


def _dsa_s_kernel(pt_ref, *refs, pp, past, topk, tdec):
    pages = refs[:pp]
    q_ref, qkv_ref, iq_ref, cb_ref, tail_ref, tri_ref, o_ref, key_s, lg_s, vt_s = refs[pp:]
    s = pl.program_id(1)
    lp = past + LANES
    wid = pp * LANES
    iw = qkv_ref[:, ROW_A:ROW_A + H_IDX]
    iqs = _head_stack(iq_ref[...], D_IDX)
    qs = (_head_stack(q_ref[...], DH) * DH ** -0.5).astype(BF16)

    def combine(r):
        sc = None
        for h in range(H_IDX):
            term = iw[:, h:h + 1] * r[h * tdec:(h + 1) * tdec]
            sc = term if sc is None else sc + term
        return sc * ((H_IDX * D_IDX) ** -0.5) + 0.0

    off = pl.multiple_of(s * wid, wid)
    ikt = _cat_pages(pages, 2 * DH, ROW_A)
    key_s[:, pl.ds(off, wid)] = _sort_key(combine(jnp.maximum(_dot3(iqs, ikt), 0.0)))
    lg_s[:, pl.ds(off, wid)] = jnp.dot(qs, _cat_pages(pages, 0, DH).astype(BF16), preferred_element_type=F32)
    vt_s[:, pl.ds(off, wid)] = _cat_pages(pages, DH, 2 * DH).astype(BF16)

    @pl.when(s == pl.num_programs(1) - 1)
    def _():
        knew = _pad_rows(qkv_ref[:, 0:DH], LANES)
        vnew = _pad_rows(qkv_ref[:, DH:2 * DH], LANES)
        iknew = _pad_rows(qkv_ref[:, 2 * DH:ROW_A], LANES)
        sc = combine(jnp.maximum(_dot3_nt(iqs, iknew), 0.0))
        causal = _lane((tdec, LANES)) <= _row((tdec, LANES))
        key_s[:, past:lp] = _sort_key(jnp.where(causal, sc, NEG))
        lg_s[:, past:lp] = _dot_nt(qs, knew)
        keys = key_s[...]
        thr = _radix_threshold(
            lambda cand: jnp.sum((keys >= cand).astype(I32), axis=1, keepdims=True), topk, tdec)
        need = (topk - jnp.sum((keys > thr).astype(I32), axis=1, keepdims=True)).astype(F32)
        eq = keys == thr
        eqf = jnp.where(eq, 1.0, 0.0)
        nck = lp // LANES
        stacked = jnp.concatenate([eqf[:, c * LANES:(c + 1) * LANES] for c in range(nck)],
                                  axis=0).astype(BF16)
        pin = jnp.dot(stacked, tri_ref[...], preferred_element_type=F32)
        tot = jnp.dot(stacked, jnp.ones((LANES, LANES), BF16), preferred_element_type=F32)
        run = jnp.zeros((tdec, LANES), F32)
        pref = []
        for c in range(nck):
            pref.append(pin[c * tdec:(c + 1) * tdec] + run)
            run = run + tot[c * tdec:(c + 1) * tdec]
        pref = jnp.concatenate(pref, axis=1)
        kpos = _lane((tdec, lp))
        sel = ((keys > thr) | (eq & (pref < need))) & ((kpos < past) | (kpos - past <= _row((tdec, lp))))
        lg = _tail_bias(lg_s[...], cb_ref, tail_ref)
        lg = lg + jnp.concatenate([jnp.where(sel, 0.0, NEG)] * H_A, axis=0)
        p = jnp.exp(lg - jnp.max(lg, axis=1, keepdims=True))
        o = (_dot_nt(p[:, :past], vt_s[...]) + _dot(p[:, past:], vnew)) / jnp.sum(p, axis=1, keepdims=True)
        o_ref[...] = _head_unstack(o, tdec)


def _sample_call(kernel, name, cache, li, pt, pp, row_inputs, const_inputs, nb, tdec, row0, scratch):
    npages = pt.shape[1]
    rb0 = row0 // tdec
    in_specs = _page_specs(cache.shape[2:], li, pp)
    args = [cache] * pp
    for arr, width, col in row_inputs:
        in_specs.append(pl.BlockSpec((tdec, width), lambda b, s, pt, col=col: (rb0 + b, col)))
        args.append(arr)
    for arr in const_inputs:
        in_specs.append(pl.BlockSpec(arr.shape, lambda b, s, pt, nd=arr.ndim: (0,) * nd))
        args.append(arr)
    return pl.pallas_call(
        kernel,
        grid_spec=pltpu.PrefetchScalarGridSpec(
            num_scalar_prefetch=1,
            grid=(nb, npages // pp),
            in_specs=in_specs,
            out_specs=pl.BlockSpec((tdec, BRANCH_W), lambda b, s, pt: (b, 0)),
            scratch_shapes=scratch),
        out_shape=jax.ShapeDtypeStruct((nb * tdec, BRANCH_W), F32),
        compiler_params=_cparams("parallel", "arbitrary"),
        name=name,
    )(pt, *args)


def _moba_s_kernel(pt_ref, *refs, pp, past, ntop, tdec):
    pages = refs[:pp]
    q_ref, kv_ref, cb_ref, tail_ref, o_ref, lg_s, vt_s, ks_s = refs[pp:]
    s = pl.program_id(1)
    lp = past + LANES
    wid = pp * LANES
    nbk = past // MOBA_BLOCK
    ppb = MOBA_BLOCK // LANES
    gw = G_B * DH
    grows = (H_B // G_B) * tdec
    rows = H_B * tdec
    qf = _head_stack(q_ref[...], DH) * DH ** -0.5
    qb = qf.astype(BF16)

    @pl.when(s == 0)
    def _():
        ks_s[...] = jnp.zeros_like(ks_s)

    off = pl.multiple_of(s * wid, wid)
    for g in range(G_B):
        ktg = _cat_pages(pages, g * DH, (g + 1) * DH).astype(BF16)
        lg_s[g * grows:(g + 1) * grows, pl.ds(off, wid)] = jnp.dot(
            qb[g * grows:(g + 1) * grows], ktg, preferred_element_type=F32)
    vt_s[:, pl.ds(off, wid)] = _cat_pages(pages, gw, 2 * gw).astype(BF16)
    blane = _lane(ks_s.shape)
    ksum = ks_s[...]
    for jb in range(pp // ppb):
        kt = pages[jb * ppb][0:gw, :]
        for j in range(1, ppb):
            kt = kt + pages[jb * ppb + j][0:gw, :]
        ksum = ksum + jnp.where(blane == s * (pp // ppb) + jb, jnp.sum(kt, axis=1, keepdims=True), 0.0)
    ks_s[...] = ksum

    @pl.when(s == pl.num_programs(1) - 1)
    def _():
        knew = kv_ref[:, 0:gw]
        vnew = kv_ref[:, gw:2 * gw]
        kmean = ks_s[...] * (1.0 / MOBA_BLOCK)
        gate = jnp.concatenate([_dot3(qf[g * grows:(g + 1) * grows], kmean[g * DH:(g + 1) * DH])
                                for g in range(G_B)], axis=0)
        n_l = _lane(gate.shape)
        gate = jnp.where(n_l < nbk, gate, NEG)
        rank = jnp.zeros(gate.shape, I32)
        for m in range(nbk):
            col = gate[:, m:m + 1]
            rank = rank + ((col > gate) | ((col == gate) & (m < n_l))).astype(I32)
        bneg = jnp.where((rank < ntop) & (gate > 0.5 * NEG), 0.0, NEG)
        trow = jnp.concatenate([_row((tdec, LANES))] * H_B, axis=0)
        neg_mask = jnp.concatenate(
            [jnp.broadcast_to(bneg[:, n:n + 1], (rows, MOBA_BLOCK)) for n in range(nbk)]
            + [jnp.where(_lane((rows, LANES)) <= trow, 0.0, NEG)], axis=1)
        for g in range(G_B):
            lg_s[g * grows:(g + 1) * grows, past:lp] = _dot_nt(
                qb[g * grows:(g + 1) * grows], _pad_rows(knew[:, g * DH:(g + 1) * DH], LANES))
        lg = _tail_bias(lg_s[...], cb_ref, tail_ref) + neg_mask
        p = jnp.exp(lg - jnp.max(lg, axis=1, keepdims=True))
        o = jnp.concatenate(
            [_dot_nt(p[g * grows:(g + 1) * grows, :past], vt_s[g * DH:(g + 1) * DH, :])
             + _dot(p[g * grows:(g + 1) * grows, past:], _pad_rows(vnew[:, g * DH:(g + 1) * DH], LANES))
             for g in range(G_B)], axis=0) / jnp.sum(p, axis=1, keepdims=True)
        o_ref[...] = _head_unstack(o, tdec)


def _mla_s_kernel(pt_ref, *refs, pp, tdec):
    pages = refs[:pp]
    ql_ref, qr_ref, rc_ref, wuv_ref, o_ref, m_s, l_s, acc_s = refs[pp:]
    s = pl.program_id(1)
    rows = H_C * tdec
    qlat, qrope = _mla_q_rows(ql_ref[...], qr_ref[...])
    scale = (D_NOPE + D_ROPE) ** -0.5

    @pl.when(s == 0)
    def _():
        m_s[...] = jnp.full(m_s.shape, -jnp.inf, F32)
        l_s[...] = jnp.zeros_like(l_s)
        acc_s[...] = jnp.zeros_like(acc_s)

    ckvt = _cat_pages(pages, 0, KV_LORA).astype(BF16)
    krt = _cat_pages(pages, KV_LORA, ROW_C).astype(BF16)
    sc = (jnp.dot(qlat, ckvt, preferred_element_type=F32) + jnp.dot(qrope, krt, preferred_element_type=F32)) * scale
    m_i, l_i, acc = _softmax_step(sc, lambda p: _dot_nt(p, ckvt), m_s[...], l_s[...], acc_s[...])
    m_s[...] = m_i
    l_s[...] = l_i
    acc_s[...] = acc

    @pl.when(s == pl.num_programs(1) - 1)
    def _():
        ckv = _pad_rows(rc_ref[:, 0:KV_LORA], LANES).astype(BF16)
        kr = _pad_rows(rc_ref[:, KV_LORA:ROW_C], LANES).astype(BF16)
        trow = jnp.concatenate([_row((tdec, LANES))] * H_C, axis=0)
        sn = (_dot_nt(qlat, ckv) + _dot_nt(qrope, kr)) * scale
        sn = jnp.where(_lane((rows, LANES)) <= trow, sn, NEG)
        _, l_f, acc_f = _softmax_step(sn, lambda p: _dot(p, ckv), m_s[...], l_s[...], acc_s[...])
        o_ref[...] = _mla_out(acc_f / l_f, wuv_ref, tdec)


def _bias_lookup(tab, bk):
    oh = jax.nn.one_hot(bk, N_BUCKETS, dtype=F32)
    return jnp.einsum('...b,bh->h...', oh, tab.astype(F32), precision=lax.Precision.HIGHEST)


def _bias_tables(tab, n_var, tq, chk, step):
    heads = tab.shape[1]
    i = jnp.arange(tq, dtype=I32)[:, None]
    j = jnp.arange(chk, dtype=I32)[None, :]
    out = []
    for r in range(n_var):
        out.append(_bias_lookup(tab, _t5_bucket(i - j + step * r)).reshape(heads * tq, chk))
    return jnp.stack(out)


def _tail_tables(tab, tdec):
    heads = tab.shape[1]
    t = jnp.arange(tdec, dtype=I32)[:, None]
    j = jnp.arange(2 * LANES, dtype=I32)[None, :]
    full = _bias_lookup(tab, _t5_bucket(t + LANES - j)).reshape(heads * tdec, 2 * LANES)
    far = tab[_t5_bucket(jnp.int32(1 << 20))].astype(F32)
    cb = jnp.repeat(far, tdec)[:, None]
    return cb, full - cb


def _pick_tile(n, cap, mult):
    best = mult
    for c in range(mult, cap + 1, mult):
        if n % c == 0:
            best = c
    return best


def kernel(x_prompt, x_sample, cache_a, cache_b, cache_c, state_d, page_table, p_prompt, p_sample, norm1_g, w_in, cq_norm_g, w_uq, ckv_norm_g, w_uk, w_uv, lb_logits, d_norm_g, t5_bias, w_branch, w_out, norm2_g, peer_wq, peer_subkeys, peer_u, peer_v, ple_gate, ple_proj, final_norm_g):
    bp, tp, d = x_prompt.shape
    bs, ts, _ = x_sample.shape
    depth = w_in.shape[0]
    npages = page_table.shape[1]
    page = cache_a.shape[2]
    past = npages * page
    n_p = bp * tp
    n_s = bs * ts
    n = n_p + n_s
    assert page == LANES and ts == 8 and tp % MOBA_BLOCK == 0 and past % MOBA_BLOCK == 0
    tm = _pick_tile(math.gcd(n_p, n_s), 256, 8)
    assert n_p % tm == 0 and n % LANES == 0
    tn_peer = _pick_tile(n, 640, LANES)
    tq = 128
    tq_a = 256
    chk = min(512, tp)
    pp = next(c for c in (16, 8, 2) if npages % c == 0)
    assert tp % chk == 0 and chk % tq_a == 0 and tp % tq_a == 0

    x = jnp.concatenate([x_prompt.reshape(n_p, d), x_sample.reshape(n_s, d)], axis=0)
    p_all = jnp.concatenate([p_prompt.reshape(depth, n_p, -1), p_sample.reshape(depth, n_s, -1)], axis=1)

    sm = jax.nn.softmax(lb_logits.astype(F32), axis=0)
    lb_all = jnp.maximum(jnp.cumsum(sm, axis=0) - sm[0], 0.0)
    zc = lambda k: jnp.zeros((depth, d, k), F32)
    o = np.cumsum((0,) + (H_A * DH, DH, DH, H_IDX * D_IDX, H_IDX, D_IDX, H_B * DH, G_B * DH, G_B * DH,
                          Q_LORA, KV_LORA, D_ROPE, 4 * H_D * DK_D))
    seg = lambda a, b: w_in[:, :, o[a]:o[b]]
    w_cat = jnp.concatenate([
        seg(12, 13),
        seg(0, 1),
        seg(1, 3), seg(5, 6), seg(4, 5), zc(256 - ROW_A - H_IDX),
        seg(6, 7), seg(7, 9),
        seg(9, 10), zc(256 - Q_LORA),
        seg(3, 4), seg(10, 11), seg(11, 12), zc(LANES - D_ROPE)], axis=2).astype(BF16)
    w_gate = w_in[:, :, o[13]:].astype(BF16)
    cqg = jnp.pad(cq_norm_g, ((0, 0), (0, 256 - Q_LORA)))[:, None, :]
    wuq3 = w_uq.reshape(depth, Q_LORA, H_C, D_NOPE + D_ROPE)
    wuq_p = jnp.concatenate([wuq3[..., :D_NOPE].reshape(depth, Q_LORA, H_C * D_NOPE),
                             wuq3[..., D_NOPE:].reshape(depth, Q_LORA, H_C * D_ROPE)], axis=2)
    wuq_p = jnp.pad(wuq_p, ((0, 0), (0, 256 - Q_LORA), (0, 0))).astype(BF16)
    wukbd = jnp.zeros((depth, H_C, D_NOPE, H_C, KV_LORA), F32)
    wuvp = jnp.zeros((depth, H_C, KV_LORA, H_C, D_V_C), F32)
    for h in range(H_C):
        wukbd = wukbd.at[:, h, :, h, :].set(jnp.swapaxes(w_uk[:, :, h, :], 1, 2))
        wuvp = wuvp.at[:, h, :, h, :].set(w_uv[:, :, h, :])
    wukbd = wukbd.reshape(depth, H_C * D_NOPE, H_C * KV_LORA).astype(BF16)
    wuvp = wuvp.reshape(depth, H_C, KV_LORA, H_C * D_V_C).astype(BF16)
    dgt = jnp.tile(d_norm_g, (1, H_D))[:, None, :]
    hd = lax.broadcasted_iota(I32, (H_D * DK_D, H_D * DK_D), 0) // DK_D
    ones_bd = (hd == hd.T).astype(BF16)
    wb = w_branch.astype(BF16)
    wo = w_out.astype(BF16)
    wq = peer_wq.astype(BF16)
    sk = peer_subkeys.reshape(depth, 2 * PEER_HEADS, N_KEYS, -1).astype(BF16)
    u_b = peer_u.astype(BF16)
    vt_b = jnp.swapaxes(peer_v, 1, 2).astype(BF16)
    pg = ple_gate.astype(BF16)
    ppj = ple_proj.astype(BF16)

    pos = jnp.concatenate([jnp.tile(jnp.arange(tp, dtype=I32), bp),
                           jnp.tile(past + jnp.arange(ts, dtype=I32), bs)])
    half = D_ROPE // 2
    inv = 1.0 / (ROPE_THETA ** (jnp.arange(half, dtype=F32) / half))
    ang = pos.astype(F32)[:, None] * inv
    cos_t = jnp.tile(jnp.cos(ang), (1, LANES // half))
    sgn = jnp.where((jnp.arange(LANES) % D_ROPE) < half, -1.0, 1.0).astype(F32)
    sin_t = jnp.tile(jnp.sin(ang), (1, LANES // half)) * sgn
    bt_a = _bias_tables(t5_bias[:, :H_A], chk // tq_a + 2, tq_a, chk, tq_a)
    bt_b = _bias_tables(t5_bias[:, H_A:], 3, MOBA_BLOCK, MOBA_BLOCK, MOBA_BLOCK)
    cb_a, tail_a = _tail_tables(t5_bias[:, :H_A], ts)
    cb_b, tail_b = _tail_tables(t5_bias[:, H_A:], ts)
    tri_c = (lax.broadcasted_iota(I32, (chk, chk), 0) < lax.broadcasted_iota(I32, (chk, chk), 1)).astype(BF16)
    tri_l = tri_c[:LANES, :LANES]
    topk_p = min(TOPK_A_MAX, tp // 4)
    topk_s = min(TOPK_A_MAX, (past + ts) // 4)
    w = H_D * DK_D
    st0_p = jnp.zeros((bp, w, w), F32)
    lp = past + LANES
    rows_dec = H_A * ts
    npool = cache_a.shape[1]
    cache_at = jnp.swapaxes(cache_a, 2, 3)
    cache_bt = jnp.transpose(cache_b, (0, 1, 3, 4, 5, 2)).reshape(depth, npool, 2 * G_B * DH, page)
    cache_ct = jnp.swapaxes(cache_c, 2, 3)
    nbl = -(-(past // MOBA_BLOCK) // LANES) * LANES

    rows_a, rows_b, rows_c, st_p, st_s = [], [], [], [], []
    y = None
    for li in range(depth):
        p = _proj_in(x, norm1_g[li][None, :], w_cat[li], tm)
        rc, ql, qr = _mla_prep(p, cos_t, sin_t, cqg[li], ckv_norm_g[li][None, :], wuq_p[li], wukbd[li], tm)
        rows_a.append(p[:, COL_AKV:COL_AKV + ROW_A])
        rows_b.append(p[:, COL_BKV:COL_BKV + 2 * G_B * DH])
        rows_c.append(rc)

        oa_p = _dsa_prompt(p, bt_a, tri_c, bp, tp, tq_a, chk, topk_p)
        ob_p = _moba_prompt(p, bt_b, bp, tp)
        oc_p = _mla_prompt(ql, qr, rc, wuvp[li], bp, tp, tq, chk)
        lbl = lb_all[li][None, :]
        od_p, stp = _hgrn(p, lbl, dgt[li], ones_bd, st0_p, bp, tp, 0, min(tp, 256), math.gcd(tp, CHUNK_D))
        st0_s = jnp.zeros((bs, H_D, DV_D, H_D, DK_D), F32)
        for h in range(H_D):
            st0_s = st0_s.at[:, h, :, h, :].set(jnp.swapaxes(state_d[li, :, h].astype(F32), 1, 2))
        st0_s = st0_s.reshape(bs, w, w)
        od_s, sts = _hgrn(p, lbl, dgt[li], ones_bd, st0_s, bs, ts, n_p, ts, math.gcd(ts, CHUNK_D))

        oa_s = _sample_call(
            functools.partial(_dsa_s_kernel, pp=pp, past=past, topk=topk_s, tdec=ts), "dsa_sample",
            cache_at, li, page_table, pp,
            [(p, 256, COL_AQ // 256), (p, 256, COL_AKV // 256), (p, 128, COL_AIQ // 128)],
            [cb_a, tail_a, tri_l], bs, ts, n_p,
            [pltpu.VMEM((ts, lp), I32), pltpu.VMEM((rows_dec, lp), F32), pltpu.VMEM((DH, past), BF16)])
        ob_s = _sample_call(
            functools.partial(_moba_s_kernel, pp=pp, past=past,
                              ntop=min(MOBA_TOPK, past // MOBA_BLOCK + 1), tdec=ts), "moba_sample",
            cache_bt, li, page_table, pp,
            [(p, 256, COL_BQ // 256), (p, 256, COL_BKV // 256)],
            [cb_b, tail_b], bs, ts, n_p,
            [pltpu.VMEM((rows_dec, lp), F32), pltpu.VMEM((G_B * DH, past), BF16),
             pltpu.VMEM((G_B * DH, nbl), F32)])
        oc_s = _sample_call(
            functools.partial(_mla_s_kernel, pp=pp, tdec=ts), "mla_sample",
            cache_ct, li, page_table, pp,
            [(ql, H_C * KV_LORA, 0), (qr, LANES, 0), (rc, ROW_C, 0)],
            [wuvp[li]], bs, ts, n_p,
            [pltpu.VMEM((rows_dec, 1), F32), pltpu.VMEM((rows_dec, 1), F32), pltpu.VMEM((rows_dec, KV_LORA), F32)])
        st_p.append(stp)
        st_s.append(sts)

        cat = lambda a, b: jnp.concatenate([a, b], axis=0)
        x1 = _merge(x, cat(oa_p, oa_s), cat(ob_p, ob_s), cat(oc_p, oc_s), cat(od_p, od_s),
                    norm1_g[li][None, :], w_gate[li], wb[li], wo[li], tm)
        g2 = norm2_g[li][None, :]
        th, s1, e0, e1 = _peer_select(x1, g2, wq[li], sk[li], _pick_tile(n, 256, LANES))
        po = _peer_dense(x1, g2, u_b[li], vt_b[li], th, s1, e0, e1, tn_peer, 8)
        x, y = _ple(x1, po, p_all[li], pg[li], ppj[li], final_norm_g[None, :], tm)

    def unstate(st, nb):
        s5 = jnp.stack(st).reshape(depth, nb, H_D, DV_D, H_D, DK_D)
        diag = jnp.stack([s5[:, :, h, :, h, :] for h in range(H_D)], axis=2)
        return jnp.swapaxes(diag, 3, 4)

    ra = jnp.stack(rows_a)
    rb = jnp.stack(rows_b)
    rcs = jnp.stack(rows_c)
    return (y[:n_p].reshape(bp, tp, d), y[n_p:].reshape(bs, ts, d),
            ra[:, :n_p].reshape(depth, bp, tp, ROW_A),
            rb[:, :n_p].reshape(depth, bp, tp, 2, G_B, DH),
            rcs[:, :n_p].reshape(depth, bp, tp, ROW_C),
            unstate(st_p, bp),
            ra[:, n_p:].reshape(depth, bs, ts, ROW_A),
            rb[:, n_p:].reshape(depth, bs, ts, 2, G_B, DH),
            rcs[:, n_p:].reshape(depth, bs, ts, ROW_C),
            unstate(st_s, bs))
```

```python
import functools
import math

import jax
import jax.numpy as jnp
import numpy as np
from jax import lax
from jax.experimental import pallas as pl
from jax.experimental.pallas import tpu as pltpu

F32 = jnp.float32
BF16 = jnp.bfloat16
I32 = jnp.int32

DH = 64
H_A = 4
H_IDX = 4
D_IDX = 32
TOPK_A_MAX = 256
H_B = 4
G_B = 2
MOBA_BLOCK = 256
MOBA_TOPK = 3
H_C = 4
Q_LORA = 192
KV_LORA = 128
D_NOPE = 64
D_ROPE = 32
D_V_C = 64
ROPE_THETA = 10000.0
H_D = 4
DK_D = 64
DV_D = 64
CHUNK_D = 16
N_BRANCH = 4
BRANCH_W = 256
N_BUCKETS = 32
MAX_DIST = 128
N_KEYS = 128
PEER_HEADS = 8
PEER_TOPK = 16
EPS = 1e-6
NEG = -1e30
F_MIN = 1e-20
ROW_A = 2 * DH + D_IDX
ROW_C = KV_LORA + D_ROPE

LANES = 128
VMEM_LIMIT = 56 * 1024 * 1024
INT_MIN = -(2 ** 31)

COL_D = 0
COL_AQ = 1024
COL_AKV = 1280
COL_BQ = 1536
COL_BKV = 1792
COL_CQ = 2048
COL_AIQ = 2304
COL_CKV = 2432
COL_CKR = 2560
P_COLS = 2688


def _cparams(*sem):
    return pltpu.CompilerParams(dimension_semantics=sem, vmem_limit_bytes=VMEM_LIMIT)


def _dot(a, b):
    return jnp.dot(a.astype(BF16), b.astype(BF16), preferred_element_type=F32)


def _dot_nt(a, b):
    return lax.dot_general(a.astype(BF16), b.astype(BF16), (((1,), (1,)), ((), ())),
                           preferred_element_type=F32)


def _split(a):
    hi = a.astype(BF16)
    lo = (a - hi.astype(F32)).astype(BF16)
    return hi, lo


def _dot3_nt(a, b):
    ah, al = _split(a)
    bh, bl = _split(b)
    dn = (((1,), (1,)), ((), ()))
    d = lambda x, y: lax.dot_general(x, y, dn, preferred_element_type=F32)
    return d(ah, bh) + (d(ah, bl) + d(al, bh))


def _rms(x, g):
    return x * lax.rsqrt(jnp.mean(x * x, axis=-1, keepdims=True) + EPS) * g


def _t5_bucket(rel):
    n = jnp.maximum(rel, 0)
    exact = N_BUCKETS // 2
    nf = jnp.maximum(n, exact).astype(F32)
    big = exact + (jnp.log(nf / exact) / math.log(MAX_DIST / exact) * (N_BUCKETS - exact)).astype(I32)
    return jnp.where(n < exact, n, jnp.minimum(big, N_BUCKETS - 1))


def _sort_key(s):
    bits = pltpu.bitcast(s, I32)
    return jnp.where(bits < 0, bits ^ jnp.int32(0x7FFFFFFF), bits)


def _lane(shape):
    return lax.broadcasted_iota(I32, shape, 1)


def _row(shape):
    return lax.broadcasted_iota(I32, shape, 0)


def _in_kernel(x_ref, g_ref, w_ref, o_ref):
    h = _rms(x_ref[...], g_ref[...])
    o_ref[...] = _dot(h, w_ref[...])


def _proj_in(x, g, w, tm):
    n, d = x.shape
    return pl.pallas_call(
        _in_kernel,
        grid=(n // tm,),
        in_specs=[pl.BlockSpec((tm, d), lambda i: (i, 0)),
                  pl.BlockSpec((1, d), lambda i: (0, 0)),
                  pl.BlockSpec((d, P_COLS), lambda i: (0, 0))],
        out_specs=pl.BlockSpec((tm, P_COLS), lambda i: (i, 0)),
        out_shape=jax.ShapeDtypeStruct((n, P_COLS), F32),
        compiler_params=_cparams("parallel"),
        name="proj_in",
    )(x, g, w)


def _rope_lanes(x, cos, sin_s):
    lane = _lane(x.shape)
    sw = jnp.where((lane % D_ROPE) < D_ROPE // 2, pltpu.roll(x, LANES - D_ROPE // 2, 1),
                   pltpu.roll(x, D_ROPE // 2, 1))
    return x * cos + sw * sin_s


def _mla_prep_kernel(cq_ref, ckv_ref, ckr_ref, cos_ref, sin_ref, cqg_ref, ckvg_ref, wuq_ref, wuk_ref,
                     rc_ref, ql_ref, qr_ref):
    cq = cq_ref[...]
    ms = jnp.sum(cq * cq, axis=-1, keepdims=True) * (1.0 / Q_LORA)
    hq = cq * lax.rsqrt(ms + EPS) * cqg_ref[...]
    qc = _dot(hq, wuq_ref[...])
    cos = cos_ref[...]
    sin_s = sin_ref[...]
    qr_ref[...] = _rope_lanes(qc[:, H_C * D_NOPE:], cos, sin_s)
    ql_ref[...] = _dot(qc[:, :H_C * D_NOPE], wuk_ref[...])
    rc_ref[:, 0:KV_LORA] = _rms(ckv_ref[...], ckvg_ref[...])
    rc_ref[:, KV_LORA:ROW_C] = _rope_lanes(ckr_ref[...], cos, sin_s)[:, 0:D_ROPE]


def _mla_prep(p, cos_t, sin_t, cqg, ckvg, wuq, wukbd, tm):
    n = p.shape[0]
    blk = lambda w, c: pl.BlockSpec((tm, w), lambda i, c=c: (i, c))
    cst = lambda a: pl.BlockSpec(a.shape, lambda i: (0,) * a.ndim)
    return pl.pallas_call(
        _mla_prep_kernel,
        grid=(n // tm,),
        in_specs=[blk(256, COL_CQ // 256), blk(128, COL_CKV // 128), blk(128, COL_CKR // 128),
                  blk(128, 0), blk(128, 0), cst(cqg), cst(ckvg), cst(wuq), cst(wukbd)],
        out_specs=[pl.BlockSpec((tm, ROW_C), lambda i: (i, 0)),
                   pl.BlockSpec((tm, H_C * KV_LORA), lambda i: (i, 0)),
                   pl.BlockSpec((tm, LANES), lambda i: (i, 0))],
        out_shape=[jax.ShapeDtypeStruct((n, ROW_C), F32),
                   jax.ShapeDtypeStruct((n, H_C * KV_LORA), F32),
                   jax.ShapeDtypeStruct((n, LANES), F32)],
        compiler_params=_cparams("parallel"),
        name="mla_prep",
    )(p, p, p, cos_t, sin_t, cqg, ckvg, wuq, wukbd)


def _merge_kernel(x_ref, oa_ref, ob_ref, oc_ref, od_ref, g_ref, wg_ref, wb_ref, wo_ref, o_ref):
    x = x_ref[...]
    h = _rms(x, g_ref[...]).astype(BF16)
    d = x.shape[1]
    mix = None
    for m, o_m in enumerate((oa_ref, ob_ref, oc_ref, od_ref)):
        gate = jax.nn.sigmoid(jnp.dot(h, wg_ref[:, m * d:(m + 1) * d], preferred_element_type=F32))
        term = gate * _dot(o_m[...], wb_ref[m])
        mix = term if mix is None else mix + term
    o_ref[...] = x + _dot(mix, wo_ref[...])


def _merge(x, oa, ob, oc, od, g, wg, wb, wo, tm):
    n, d = x.shape
    row = lambda w: pl.BlockSpec((tm, w), lambda i: (i, 0))
    cst = lambda a: pl.BlockSpec(a.shape, lambda i: (0,) * a.ndim)
    return pl.pallas_call(
        _merge_kernel,
        grid=(n // tm,),
        in_specs=[row(d), row(BRANCH_W), row(BRANCH_W), row(BRANCH_W), row(BRANCH_W),
                  cst(g), cst(wg), cst(wb), cst(wo)],
        out_specs=row(d),
        out_shape=jax.ShapeDtypeStruct((n, d), F32),
        compiler_params=_cparams("parallel"),
        name="merge",
    )(x, oa, ob, oc, od, g, wg, wb, wo)


def _ple_kernel(x_ref, po_ref, p_ref, pg_ref, pp_ref, gf_ref, o_ref, y_ref):
    x2 = x_ref[...] + po_ref[...]
    xn = x2 + jax.nn.sigmoid(_dot(x2, pg_ref[...])) * _dot(p_ref[...], pp_ref[...])
    o_ref[...] = xn
    y_ref[...] = _rms(xn, gf_ref[...])


def _ple(x, po, p, pg, pp, gf, tm):
    n, d = x.shape
    row = lambda w: pl.BlockSpec((tm, w), lambda i: (i, 0))
    cst = lambda a: pl.BlockSpec(a.shape, lambda i: (0,) * a.ndim)
    return pl.pallas_call(
        _ple_kernel,
        grid=(n // tm,),
        in_specs=[row(d), row(d), row(p.shape[1]), cst(pg), cst(pp), cst(gf)],
        out_specs=[row(d), row(d)],
        out_shape=[jax.ShapeDtypeStruct((n, d), F32), jax.ShapeDtypeStruct((n, d), F32)],
        compiler_params=_cparams("parallel"),
        name="ple",
    )(x, po, p, pg, pp, gf)


PEER_NCAND = 80
PEER_NTOP = 24


def _extract_top(w_s, out_s, n_chains, count):
    def body(r, carry):
        for c in range(n_chains):
            w = w_s[c]
            m = jnp.max(w, axis=0, keepdims=True)
            out_s[c, pl.ds(r, 1), :] = m
            w_s[c] = jnp.where(w == m, -jnp.inf, w)
        return carry
    lax.fori_loop(0, count, body, 0)


def _peer_sel_kernel(x_ref, g_ref, wq_ref, sk_ref, th_ref, s1_ref, e0_ref, e1_ref, w_s, sv_s, c_s, top_s):
    hn = _rms(x_ref[...], g_ref[...])
    q = _dot(hn, wq_ref[...])
    for hp in range(2 * PEER_HEADS):
        s = _dot_nt(sk_ref[hp], q[:, hp * N_KEYS:(hp + 1) * N_KEYS])
        w_s[hp] = s
        if hp % 2 == 0:
            th_ref[hp // 2] = s
        else:
            s1_ref[hp // 2] = s
    _extract_top(w_s, sv_s, 2 * PEER_HEADS, PEER_TOPK)
    half = PEER_TOPK // 2
    for h in range(PEER_HEADS):
        sv0 = sv_s[2 * h]
        sv1 = sv_s[2 * h + 1]
        pieces = [sv0[0:1, :] + sv1]
        pieces += [sv0[a:a + 1, :] + sv1[0:half, :] for a in range(1, half)]
        pieces.append(sv0[half:PEER_TOPK, :] + sv1[0:1, :])
        c_s[h] = jnp.concatenate(pieces, axis=0)
    _extract_top(c_s, top_s, PEER_HEADS, PEER_TOPK + 1)
    for h in range(PEER_HEADS):
        top = top_s[h]
        z = jnp.sum(jnp.exp(top[0:PEER_TOPK, :] - top[0:1, :]), axis=0, keepdims=True)
        cut = 0.5 * (top[PEER_TOPK - 1:PEER_TOPK, :] + top[PEER_TOPK:PEER_TOPK + 1, :])
        s0 = th_ref[h]
        e0_ref[h] = jnp.exp(s0 - sv_s[2 * h, 0:1, :]) / z
        e1_ref[h] = jnp.exp(s1_ref[h] - sv_s[2 * h + 1, 0:1, :])
        th_ref[h] = cut - s0


def _peer_select(x, g, wq, sk, tn):
    n, d = x.shape
    hk = jax.ShapeDtypeStruct((PEER_HEADS, N_KEYS, n), F32)
    hk_spec = pl.BlockSpec((PEER_HEADS, N_KEYS, tn), lambda i: (0, 0, i))
    cst = lambda a: pl.BlockSpec(a.shape, lambda i: (0,) * a.ndim)
    return pl.pallas_call(
        _peer_sel_kernel,
        grid=(n // tn,),
        in_specs=[pl.BlockSpec((tn, d), lambda i: (i, 0)), cst(g), cst(wq), cst(sk)],
        out_specs=[hk_spec, hk_spec, hk_spec, hk_spec],
        out_shape=[hk, hk, hk, hk],
        scratch_shapes=[pltpu.VMEM((2 * PEER_HEADS, N_KEYS, tn), F32),
                        pltpu.VMEM((2 * PEER_HEADS, PEER_TOPK, tn), F32),
                        pltpu.VMEM((PEER_HEADS, PEER_NCAND, tn), F32),
                        pltpu.VMEM((PEER_HEADS, PEER_NTOP, tn), F32)],
        compiler_params=_cparams("parallel"),
        name="peer_select",
    )(x, g, wq, sk)


def _peer_dense_kernel(x_ref, g_ref, u_ref, vt_ref, th_ref, s1_ref, e0_ref, e1_ref, o_ref, hn_s, acc_s,
                       act_s, z_s, *, ic):
    c = pl.program_id(1)
    tn = hn_s.shape[1]

    @pl.when(c == 0)
    def _():
        hn_s[...] = _rms(x_ref[...], g_ref[...]).T.astype(BF16)
        acc_s[...] = jnp.zeros_like(acc_s)
        act_s[...] = jnp.zeros_like(act_s)
        z_s[...] = jnp.zeros_like(z_s)

    cur = c % 2
    prv = 1 - cur
    for ii in range(ic):
        if ii == 1:
            act_s[cur] = jnp.dot(u_ref[...], hn_s[...], preferred_element_type=F32)
        if ii == ic // 2:
            acc_s[...] += jnp.dot(vt_ref[...], z_s[cur], preferred_element_type=F32)
        er = slice(ii * N_KEYS, (ii + 1) * N_KEYS)
        for lt in range(tn // LANES):
            tok = slice(lt * LANES, (lt + 1) * LANES)
            wt = None
            for h in range(PEER_HEADS):
                t = jnp.where(s1_ref[h, :, tok] >= th_ref[h, ii:ii + 1, tok], e1_ref[h, :, tok], 0.0)
                term = e0_ref[h, ii:ii + 1, tok] * t
                wt = term if wt is None else wt + term
            z_s[prv, er, tok] = (wt * jax.nn.gelu(act_s[prv, er, tok])).astype(BF16)

    @pl.when(c == pl.num_programs(1) - 1)
    def _():
        o_ref[...] = acc_s[...].T


def _peer_dense(x, g, u, vt, th, s1, e0, e1, tn, ic):
    n, d = x.shape
    ne = u.shape[0]
    ec = ic * N_KEYS
    nc = ne // ec
    clamp = lambda c: jnp.minimum(jnp.maximum(c, 0), nc - 1)
    return pl.pallas_call(
        functools.partial(_peer_dense_kernel, ic=ic),
        grid=(n // tn, nc + 2),
        in_specs=[pl.BlockSpec((tn, d), lambda i, c: (i, 0)),
                  pl.BlockSpec((1, d), lambda i, c: (0, 0)),
                  pl.BlockSpec((ec, d), lambda i, c: (clamp(c), 0)),
                  pl.BlockSpec((d, ec), lambda i, c: (0, clamp(c - 2))),
                  pl.BlockSpec((PEER_HEADS, ic, tn), lambda i, c: (0, clamp(c - 1), i)),
                  pl.BlockSpec((PEER_HEADS, N_KEYS, tn), lambda i, c: (0, 0, i)),
                  pl.BlockSpec((PEER_HEADS, ic, tn), lambda i, c: (0, clamp(c - 1), i)),
                  pl.BlockSpec((PEER_HEADS, N_KEYS, tn), lambda i, c: (0, 0, i))],
        out_specs=pl.BlockSpec((tn, d), lambda i, c: (i, 0)),
        out_shape=jax.ShapeDtypeStruct((n, d), F32),
        scratch_shapes=[pltpu.VMEM((d, tn), BF16), pltpu.VMEM((d, tn), F32),
                        pltpu.VMEM((2, ec, tn), F32), pltpu.VMEM((2, ec, tn), BF16)],
        compiler_params=_cparams("parallel", "arbitrary"),
        name="peer_dense",
    )(x, g, u, vt, th, s1, e0, e1)


def _hgrn_kernel(p_ref, lb_ref, dg_ref, ones_ref, st0_ref, o_ref, st_ref, q_s, k_s, b_s, v_s, o_s,
                 *, ts, ch):
    s = pl.program_id(1)
    w = H_D * DK_D

    @pl.when(s == 0)
    def _():
        st_ref[...] = st0_ref[...]

    dq = p_ref[:, 0:w]
    df = p_ref[:, w:2 * w]
    lb = lb_ref[...]
    sig = jax.nn.sigmoid(df)
    logf = jnp.log(jnp.maximum(lb + (1.0 - lb) * sig, F_MIN))
    r = _row((ts, w)) % ch
    b = logf
    sh = 1
    while sh < ch:
        b = b + jnp.where(r >= sh, pltpu.roll(b, sh, 0), 0.0)
        sh *= 2
    q_s[...] = dq * jax.nn.sigmoid(dq)
    k_s[...] = (1.0 - lb) * (1.0 - sig)
    b_s[...] = b
    v_s[...] = p_ref[:, 2 * w:3 * w]
    ones_bd = ones_ref[...]
    bd_mask = ones_bd.astype(F32)
    srow = _row((ch, w))

    def body(c, carry):
        off = pl.multiple_of(c * ch, ch)
        q = q_s[pl.ds(off, ch), :]
        kk = k_s[pl.ds(off, ch), :]
        bb = b_s[pl.ds(off, ch), :]
        v = v_s[pl.ds(off, ch), :]
        st = st_ref[0]
        o_inter = _dot_nt(q * jnp.exp(bb), st)
        rows = []
        for t in range(ch):
            dec = jnp.exp(jnp.where(srow <= t, bb[t:t + 1, :] - bb, 0.0))
            rows.append(jnp.where(srow <= t, dec * kk * q[t:t + 1, :], 0.0))
        pm = jnp.concatenate(rows, axis=0)
        rr = jnp.dot(pm.astype(BF16), ones_bd, preferred_element_type=F32)
        o_intra = jnp.sum(rr.reshape(ch, ch, w) * v[None, :, :], axis=1)
        o_s[pl.ds(off, ch), :] = o_inter + o_intra
        bl = bb[ch - 1:ch, :]
        kdec = kk * jnp.exp(bl - bb)
        upd = lax.dot_general(v.astype(BF16), kdec.astype(BF16), (((0,), (0,)), ((), ())),
                              preferred_element_type=F32)
        st_ref[0] = (st * jnp.exp(bl) + upd) * bd_mask
        return carry

    lax.fori_loop(0, ts // ch, body, 0)
    o = o_s[...]
    ms = jnp.dot((o * o).astype(BF16), ones_bd, preferred_element_type=F32) * (1.0 / DV_D)
    dg = p_ref[:, 3 * w:4 * w]
    o_ref[...] = o * lax.rsqrt(ms + EPS) * dg_ref[...] * (dg * jax.nn.sigmoid(dg))


def _hgrn(p, lb, dg, ones_bd, st0, nb, t, row0, ts, ch):
    w = H_D * DK_D
    ns = t // ts
    rb0 = row0 // ts
    cst = lambda a: pl.BlockSpec(a.shape, lambda b, s: (0,) * a.ndim)
    return pl.pallas_call(
        functools.partial(_hgrn_kernel, ts=ts, ch=ch),
        grid=(nb, ns),
        in_specs=[pl.BlockSpec((ts, 4 * w), lambda b, s: (rb0 + b * ns + s, COL_D // (4 * w))),
                  cst(lb), cst(dg), cst(ones_bd),
                  pl.BlockSpec((1, w, w), lambda b, s: (b, 0, 0))],
        out_specs=[pl.BlockSpec((ts, w), lambda b, s: (b * ns + s, 0)),
                   pl.BlockSpec((1, w, w), lambda b, s: (b, 0, 0))],
        out_shape=[jax.ShapeDtypeStruct((nb * t, w), F32), jax.ShapeDtypeStruct((nb, w, w), F32)],
        scratch_shapes=[pltpu.VMEM((ts, w), F32)] * 5,
        compiler_params=_cparams("parallel", "arbitrary"),
        name="hgrn",
    )(p, lb, dg, ones_bd, st0)


def _radix_threshold(count_ge, k, rows):
    def body(i, t):
        cand = t ^ jnp.left_shift(jnp.int32(1), 31 - i)
        return jnp.where(count_ge(cand) >= k, cand, t)
    return lax.fori_loop(0, 32, body, jnp.full((rows, 1), INT_MIN, I32))


def _head_rows_low(q, scale):
    lane = _lane((q.shape[0], LANES))
    out = []
    for m in range(2):
        tile = q[:, m * LANES:(m + 1) * LANES] * scale
        out.append(jnp.where(lane < DH, tile, 0.0))
        out.append(jnp.where(lane < DH, pltpu.roll(tile, DH, 1), 0.0))
    return jnp.concatenate(out, axis=0)


def _head_rows_group(q, scale):
    lane = _lane((q.shape[0], LANES))
    t0 = q[:, 0:LANES] * scale
    t1 = q[:, LANES:2 * LANES] * scale
    return jnp.concatenate([
        jnp.where(lane < DH, t0, 0.0),
        jnp.where(lane < DH, pltpu.roll(t0, DH, 1), 0.0),
        jnp.where(lane >= DH, pltpu.roll(t1, DH, 1), 0.0),
        jnp.where(lane >= DH, t1, 0.0)], axis=0)


def _idx_heads(iq):
    lane = _lane(iq.shape)
    return [jnp.where(lane < D_IDX, iq if h == 0 else pltpu.roll(iq, LANES - D_IDX * h, 1), 0.0)
            for h in range(H_IDX)]


def _index_score(iqh, iw, ik_tile):
    s = None
    for h in range(H_IDX):
        r = jnp.maximum(_dot3_nt(iqh[h], ik_tile), 0.0)
        term = iw[:, h:h + 1] * r
        s = term if s is None else s + term
    return s * ((H_IDX * D_IDX) ** -0.5) + 0.0


def _softmax_step(lg, pv, m_i, l_i, acc):
    m_new = jnp.maximum(m_i, jnp.max(lg, axis=1, keepdims=True))
    alpha = jnp.exp(m_i - m_new)
    p = jnp.exp(lg - m_new)
    l_new = alpha * l_i + jnp.sum(p, axis=1, keepdims=True)
    return m_new, l_new, alpha * acc + pv(p)


def _dsa_p_kernel(q_ref, qkv_ref, iq_ref, kv_ref, bt_ref, tri_ref, o_ref, key_s, *, tq, chk, topk):
    qi = pl.program_id(1)
    nch = (qi * tq + tq + chk - 1) // chk
    qpos = qi * tq + _row((tq, 1))
    iw = qkv_ref[:, ROW_A:ROW_A + H_IDX]
    iqh = _idx_heads(iq_ref[...])
    nv = bt_ref.shape[0]

    def kpos_of(off):
        return off + _lane((tq, chk))

    def score_body(c, carry):
        off = pl.multiple_of(c * chk, chk)
        s = _index_score(iqh, iw, kv_ref[pl.ds(off, chk), LANES:2 * LANES])
        s = jnp.where(kpos_of(off) <= qpos, s, NEG)
        key_s[:, pl.ds(off, chk)] = _sort_key(s)
        return carry

    lax.fori_loop(0, nch, score_body, 0)

    rblk = 64

    def count(pred, ref_val):
        def body(c, acc):
            off = pl.multiple_of(c * chk, chk)
            out = []
            for rb in range(tq // rblk):
                rows = slice(rb * rblk, (rb + 1) * rblk)
                a = acc[rows]
                rv = ref_val[rows]
                for l in range(chk // LANES):
                    kc = key_s[rows, pl.ds(pl.multiple_of(off + l * LANES, LANES), LANES)]
                    a = a + pred(kc, rv).astype(I32)
                out.append(a)
            return jnp.concatenate(out, axis=0)
        acc = lax.fori_loop(0, nch, body, jnp.zeros((tq, LANES), I32))
        return jnp.sum(acc, axis=1, keepdims=True)

    thr = _radix_threshold(lambda cand: count(lambda kc, cd: kc >= cd, cand), topk, tq)
    need = (topk - count(lambda kc, t: kc > t, thr)).astype(F32)

    qs = _head_rows_low(q_ref[...], DH ** -0.5).astype(BF16)

    def att_body(c, carry):
        m_i, l_i, acc, run = carry
        off = pl.multiple_of(c * chk, chk)
        kc = key_s[:, pl.ds(off, chk)]
        eq = kc == thr
        eqf = jnp.where(eq, 1.0, 0.0)
        pref = jnp.dot(eqf.astype(BF16), tri_ref[...], preferred_element_type=F32) + run
        sel = (kc > thr) | (eq & (pref < need))
        kvt = kv_ref[pl.ds(off, chk), 0:LANES].astype(BF16)
        lg = lax.dot_general(qs, kvt, (((1,), (1,)), ((), ())), preferred_element_type=F32)
        r = jnp.minimum(qi - c * (chk // tq), nv - 1)
        lg = lg + bt_ref[r]
        lg = lg + jnp.concatenate([jnp.where(sel, 0.0, NEG)] * H_A, axis=0)
        m_i, l_i, acc = _softmax_step(lg, lambda p: _dot(p, kvt), m_i, l_i, acc)
        return m_i, l_i, acc, run + jnp.sum(eqf, axis=1, keepdims=True)

    init = (jnp.full((H_A * tq, 1), -jnp.inf, F32), jnp.zeros((H_A * tq, 1), F32),
            jnp.zeros((H_A * tq, LANES), F32), jnp.zeros((tq, 1), F32))
    _, l_i, acc, _ = lax.fori_loop(0, nch, att_body, init)
    o = acc / l_i
    lane = _lane((tq, LANES))
    for m in range(2):
        o_ref[:, m * LANES:(m + 1) * LANES] = jnp.where(
            lane < DH, pltpu.roll(o[2 * m * tq:(2 * m + 1) * tq], DH, 1), o[(2 * m + 1) * tq:(2 * m + 2) * tq])


def _dsa_prompt(p, bt, tri, nb, t, tq, chk, topk):
    nq = t // tq
    return pl.pallas_call(
        functools.partial(_dsa_p_kernel, tq=tq, chk=chk, topk=topk),
        grid=(nb, nq),
        in_specs=[pl.BlockSpec((tq, 256), lambda b, i: (b * nq + i, COL_AQ // 256)),
                  pl.BlockSpec((tq, 256), lambda b, i: (b * nq + i, COL_AKV // 256)),
                  pl.BlockSpec((tq, 128), lambda b, i: (b * nq + i, COL_AIQ // 128)),
                  pl.BlockSpec((t, 256), lambda b, i: (b, COL_AKV // 256)),
                  pl.BlockSpec(bt.shape, lambda b, i: (0, 0, 0)),
                  pl.BlockSpec(tri.shape, lambda b, i: (0, 0))],
        out_specs=pl.BlockSpec((tq, BRANCH_W), lambda b, i: (b * nq + i, 0)),
        out_shape=jax.ShapeDtypeStruct((nb * t, BRANCH_W), F32),
        scratch_shapes=[pltpu.VMEM((tq, t), I32)],
        compiler_params=_cparams("parallel", "arbitrary"),
        name="dsa_prompt",
    )(p, p, p, p, bt, tri)


def _moba_p_kernel(q_ref, kv_ref, bt_ref, o_ref, m_s, l_s, acc_s, *, nblk, ntop):
    qi = pl.program_id(1)
    tq = MOBA_BLOCK
    rows = H_B * tq
    qs = _head_rows_group(q_ref[...], DH ** -0.5)
    kmean = jnp.sum(kv_ref[:, 0:LANES].reshape(nblk, MOBA_BLOCK, LANES), axis=1) * (1.0 / MOBA_BLOCK)
    gate = _dot3_nt(qs, kmean)
    n_l = _lane((rows, nblk))
    gate = jnp.where(n_l < qi, gate, NEG)
    rank = jnp.zeros((rows, nblk), I32)
    for m in range(nblk):
        col = gate[:, m:m + 1]
        beats = (col > gate) | ((col == gate) & (m < n_l))
        rank = rank + beats.astype(I32)
    bm = ((rank < ntop) & (gate > 0.5 * NEG)) | (n_l == qi)
    bneg = jnp.where(bm, 0.0, NEG)
    m_s[...] = jnp.full(m_s.shape, -jnp.inf, F32)
    l_s[...] = jnp.zeros(l_s.shape, F32)
    acc_s[...] = jnp.zeros(acc_s.shape, F32)
    qb = qs.astype(BF16)
    for n in range(nblk):
        @pl.when(n <= qi)
        def _(n=n):
            kt = kv_ref[n * MOBA_BLOCK:(n + 1) * MOBA_BLOCK, 0:LANES].astype(BF16)
            vt = kv_ref[n * MOBA_BLOCK:(n + 1) * MOBA_BLOCK, LANES:2 * LANES].astype(BF16)
            lg = lax.dot_general(qb, kt, (((1,), (1,)), ((), ())), preferred_element_type=F32)
            lg = lg + bt_ref[jnp.minimum(qi - n, bt_ref.shape[0] - 1)] + bneg[:, n:n + 1]
            m_i, l_i, acc = _softmax_step(lg, lambda p: _dot(p, vt), m_s[...], l_s[...], acc_s[...])
            m_s[...] = m_i
            l_s[...] = l_i
            acc_s[...] = acc
    o = acc_s[...] / l_s[...]
    lane = _lane((tq, LANES))
    o_ref[:, 0:LANES] = jnp.where(lane < DH, o[0:tq], pltpu.roll(o[tq:2 * tq], DH, 1))
    o_ref[:, LANES:2 * LANES] = jnp.where(lane < DH, pltpu.roll(o[2 * tq:3 * tq], DH, 1), o[3 * tq:4 * tq])


def _moba_prompt(p, bt, nb, t):
    tq = MOBA_BLOCK
    nq = t // tq
    rows = H_B * tq
    return pl.pallas_call(
        functools.partial(_moba_p_kernel, nblk=nq, ntop=min(MOBA_TOPK, nq)),
        grid=(nb, nq),
        in_specs=[pl.BlockSpec((tq, 256), lambda b, i: (b * nq + i, COL_BQ // 256)),
                  pl.BlockSpec((t, 256), lambda b, i: (b, COL_BKV // 256)),
                  pl.BlockSpec(bt.shape, lambda b, i: (0, 0, 0))],
        out_specs=pl.BlockSpec((tq, BRANCH_W), lambda b, i: (b * nq + i, 0)),
        out_shape=jax.ShapeDtypeStruct((nb * t, BRANCH_W), F32),
        scratch_shapes=[pltpu.VMEM((rows, 1), F32), pltpu.VMEM((rows, 1), F32), pltpu.VMEM((rows, LANES), F32)],
        compiler_params=_cparams("parallel", "arbitrary"),
        name="moba_prompt",
    )(p, p, bt)


def _mla_q_rows(ql, qr):
    qlat = jnp.concatenate([ql[:, h * KV_LORA:(h + 1) * KV_LORA] for h in range(H_C)], axis=0)
    qrope = jnp.concatenate([qr[:, h * D_ROPE:(h + 1) * D_ROPE] for h in range(H_C)], axis=0)
    return qlat.astype(BF16), qrope.astype(BF16)


def _mla_out(o_lat, wuv_ref, t):
    out = None
    for h in range(H_C):
        term = _dot(o_lat[h * t:(h + 1) * t], wuv_ref[h])
        out = term if out is None else out + term
    return out


def _mla_p_kernel(ql_ref, qr_ref, rc_ref, wuv_ref, o_ref, *, tq, chk):
    qi = pl.program_id(1)
    nch = (qi * tq + tq + chk - 1) // chk
    rows = H_C * tq
    qlat, qrope = _mla_q_rows(ql_ref[...], qr_ref[...])
    qpos = qi * tq + _row((tq, 1))
    qpos4 = jnp.concatenate([qpos] * H_C, axis=0)
    scale = (D_NOPE + D_ROPE) ** -0.5
    dn = (((1,), (1,)), ((), ()))

    def body(c, carry):
        m_i, l_i, acc = carry
        off = pl.multiple_of(c * chk, chk)
        ckv = rc_ref[pl.ds(off, chk), 0:KV_LORA].astype(BF16)
        kr = rc_ref[pl.ds(off, chk), KV_LORA:ROW_C].astype(BF16)
        s = (lax.dot_general(qlat, ckv, dn, preferred_element_type=F32)
             + lax.dot_general(qrope, kr, dn, preferred_element_type=F32)) * scale
        kpos = off + _lane((rows, chk))
        s = jnp.where(kpos <= qpos4, s, NEG)
        return _softmax_step(s, lambda p: _dot(p, ckv), m_i, l_i, acc)

    init = (jnp.full((rows, 1), -jnp.inf, F32), jnp.zeros((rows, 1), F32), jnp.zeros((rows, KV_LORA), F32))
    _, l_i, acc = lax.fori_loop(0, nch, body, init)
    o_ref[...] = _mla_out(acc / l_i, wuv_ref, tq)


def _mla_prompt(ql, qr, rc, wuvp, nb, t, tq, chk):
    nq = t // tq
    return pl.pallas_call(
        functools.partial(_mla_p_kernel, tq=tq, chk=chk),
        grid=(nb, nq),
        in_specs=[pl.BlockSpec((tq, H_C * KV_LORA), lambda b, i: (b * nq + i, 0)),
                  pl.BlockSpec((tq, LANES), lambda b, i: (b * nq + i, 0)),
                  pl.BlockSpec((t, ROW_C), lambda b, i: (b, 0)),
                  pl.BlockSpec(wuvp.shape, lambda b, i: (0, 0, 0))],
        out_specs=pl.BlockSpec((tq, BRANCH_W), lambda b, i: (b * nq + i, 0)),
        out_shape=jax.ShapeDtypeStruct((nb * t, BRANCH_W), F32),
        compiler_params=_cparams("parallel", "arbitrary"),
        name="mla_prompt",
    )(ql, qr, rc, wuvp)


def _page_specs(shape_tail, li, pp):
    nd = len(shape_tail)
    return [pl.BlockSpec((None, None) + shape_tail,
                         lambda b, s, pt, j=j: (li, pt[b, s * pp + j]) + (0,) * nd)
            for j in range(pp)]


def _cat_pages(pages, lo, hi):
    return jnp.concatenate([pg[lo:hi, :] for pg in pages], axis=1)


def _pad_rows(x, rows):
    return jnp.concatenate([x, jnp.zeros((rows - x.shape[0], x.shape[1]), x.dtype)], axis=0)


def _head_stack(x, width):
    return jnp.concatenate([x[:, h * width:(h + 1) * width] for h in range(x.shape[1] // width)], axis=0)


def _head_unstack(o, t):
    return jnp.concatenate([o[h * t:(h + 1) * t] for h in range(o.shape[0] // t)], axis=1)


def _dot3(a, b):
    ah, al = _split(a)
    bh, bl = _split(b)
    d = lambda x, y: jnp.dot(x, y, preferred_element_type=F32)
    return d(ah, bh) + (d(ah, bl) + d(al, bh))


def _tail_bias(lg, cb_ref, tail_ref):
    lg = lg + cb_ref[...]
    w = lg.shape[1] - 2 * LANES
    return jnp.concatenate([lg[:, :w], lg[:, w:] + tail_ref[...]], axis=1)


def _dsa_s_kernel(pt_ref, *refs, pp, past, topk, tdec):
    pages = refs[:pp]
    q_ref, qkv_ref, iq_ref, cb_ref, tail_ref, tri_ref, o_ref, key_s, lg_s, vt_s = refs[pp:]
    s = pl.program_id(1)
    lp = past + LANES
    wid = pp * LANES
    iw = qkv_ref[:, ROW_A:ROW_A + H_IDX]
    iqs = _head_stack(iq_ref[...], D_IDX)
    qs = (_head_stack(q_ref[...], DH) * DH ** -0.5).astype(BF16)

    def combine(r):
        sc = None
        for h in range(H_IDX):
            term = iw[:, h:h + 1] * r[h * tdec:(h + 1) * tdec]
            sc = term if sc is None else sc + term
        return sc * ((H_IDX * D_IDX) ** -0.5) + 0.0

    off = pl.multiple_of(s * wid, wid)
    ikt = _cat_pages(pages, 2 * DH, ROW_A)
    key_s[:, pl.ds(off, wid)] = _sort_key(combine(jnp.maximum(_dot3(iqs, ikt), 0.0)))
    lg_s[:, pl.ds(off, wid)] = jnp.dot(qs, _cat_pages(pages, 0, DH).astype(BF16), preferred_element_type=F32)
    vt_s[:, pl.ds(off, wid)] = _cat_pages(pages, DH, 2 * DH).astype(BF16)

    @pl.when(s == pl.num_programs(1) - 1)
    def _():
        knew = _pad_rows(qkv_ref[:, 0:DH], LANES)
        vnew = _pad_rows(qkv_ref[:, DH:2 * DH], LANES)
        iknew = _pad_rows(qkv_ref[:, 2 * DH:ROW_A], LANES)
        sc = combine(jnp.maximum(_dot3_nt(iqs, iknew), 0.0))
        causal = _lane((tdec, LANES)) <= _row((tdec, LANES))
        key_s[:, past:lp] = _sort_key(jnp.where(causal, sc, NEG))
        lg_s[:, past:lp] = _dot_nt(qs, knew)
        keys = key_s[...]
        thr = _radix_threshold(
            lambda cand: jnp.sum((keys >= cand).astype(I32), axis=1, keepdims=True), topk, tdec)
        need = (topk - jnp.sum((keys > thr).astype(I32), axis=1, keepdims=True)).astype(F32)
        eq = keys == thr
        eqf = jnp.where(eq, 1.0, 0.0)
        nck = lp // LANES
        stacked = jnp.concatenate([eqf[:, c * LANES:(c + 1) * LANES] for c in range(nck)],
                                  axis=0).astype(BF16)
        pin = jnp.dot(stacked, tri_ref[...], preferred_element_type=F32)
        tot = jnp.dot(stacked, jnp.ones((LANES, LANES), BF16), preferred_element_type=F32)
        run = jnp.zeros((tdec, LANES), F32)
        pref = []
        for c in range(nck):
            pref.append(pin[c * tdec:(c + 1) * tdec] + run)
            run = run + tot[c * tdec:(c + 1) * tdec]
        pref = jnp.concatenate(pref, axis=1)
        kpos = _lane((tdec, lp))
        sel = ((keys > thr) | (eq & (pref < need))) & ((kpos < past) | (kpos - past <= _row((tdec, lp))))
        lg = _tail_bias(lg_s[...], cb_ref, tail_ref)
        lg = lg + jnp.concatenate([jnp.where(sel, 0.0, NEG)] * H_A, axis=0)
        p = jnp.exp(lg - jnp.max(lg, axis=1, keepdims=True))
        o = (_dot_nt(p[:, :past], vt_s[...]) + _dot(p[:, past:], vnew)) / jnp.sum(p, axis=1, keepdims=True)
        o_ref[...] = _head_unstack(o, tdec)


def _sample_call(kernel, name, cache, li, pt, pp, row_inputs, const_inputs, nb, tdec, row0, scratch):
    npages = pt.shape[1]
    rb0 = row0 // tdec
    in_specs = _page_specs(cache.shape[2:], li, pp)
    args = [cache] * pp
    for arr, width, col in row_inputs:
        in_specs.append(pl.BlockSpec((tdec, width), lambda b, s, pt, col=col: (rb0 + b, col)))
        args.append(arr)
    for arr in const_inputs:
        in_specs.append(pl.BlockSpec(arr.shape, lambda b, s, pt, nd=arr.ndim: (0,) * nd))
        args.append(arr)
    return pl.pallas_call(
        kernel,
        grid_spec=pltpu.PrefetchScalarGridSpec(
            num_scalar_prefetch=1,
            grid=(nb, npages // pp),
            in_specs=in_specs,
            out_specs=pl.BlockSpec((tdec, BRANCH_W), lambda b, s, pt: (b, 0)),
            scratch_shapes=scratch),
        out_shape=jax.ShapeDtypeStruct((nb * tdec, BRANCH_W), F32),
        compiler_params=_cparams("parallel", "arbitrary"),
        name=name,
    )(pt, *args)


def _moba_s_kernel(pt_ref, *refs, pp, past, ntop, tdec):
    pages = refs[:pp]
    q_ref, kv_ref, cb_ref, tail_ref, o_ref, lg_s, vt_s, ks_s = refs[pp:]
    s = pl.program_id(1)
    lp = past + LANES
    wid = pp * LANES
    nbk = past // MOBA_BLOCK
    ppb = MOBA_BLOCK // LANES
    gw = G_B * DH
    grows = (H_B // G_B) * tdec
    rows = H_B * tdec
    qf = _head_stack(q_ref[...], DH) * DH ** -0.5
    qb = qf.astype(BF16)

    @pl.when(s == 0)
    def _():
        ks_s[...] = jnp.zeros_like(ks_s)

    off = pl.multiple_of(s * wid, wid)
    for g in range(G_B):
        ktg = _cat_pages(pages, g * DH, (g + 1) * DH).astype(BF16)
        lg_s[g * grows:(g + 1) * grows, pl.ds(off, wid)] = jnp.dot(
            qb[g * grows:(g + 1) * grows], ktg, preferred_element_type=F32)
    vt_s[:, pl.ds(off, wid)] = _cat_pages(pages, gw, 2 * gw).astype(BF16)
    blane = _lane(ks_s.shape)
    ksum = ks_s[...]
    for jb in range(pp // ppb):
        kt = pages[jb * ppb][0:gw, :]
        for j in range(1, ppb):
            kt = kt + pages[jb * ppb + j][0:gw, :]
        ksum = ksum + jnp.where(blane == s * (pp // ppb) + jb, jnp.sum(kt, axis=1, keepdims=True), 0.0)
    ks_s[...] = ksum

    @pl.when(s == pl.num_programs(1) - 1)
    def _():
        knew = kv_ref[:, 0:gw]
        vnew = kv_ref[:, gw:2 * gw]
        kmean = ks_s[...] * (1.0 / MOBA_BLOCK)
        gate = jnp.concatenate([_dot3(qf[g * grows:(g + 1) * grows], kmean[g * DH:(g + 1) * DH])
                                for g in range(G_B)], axis=0)
        n_l = _lane(gate.shape)
        gate = jnp.where(n_l < nbk, gate, NEG)
        rank = jnp.zeros(gate.shape, I32)
        for m in range(nbk):
            col = gate[:, m:m + 1]
            rank = rank + ((col > gate) | ((col == gate) & (m < n_l))).astype(I32)
        bneg = jnp.where((rank < ntop) & (gate > 0.5 * NEG), 0.0, NEG)
        trow = jnp.concatenate([_row((tdec, LANES))] * H_B, axis=0)
        neg_mask = jnp.concatenate(
            [jnp.broadcast_to(bneg[:, n:n + 1], (rows, MOBA_BLOCK)) for n in range(nbk)]
            + [jnp.where(_lane((rows, LANES)) <= trow, 0.0, NEG)], axis=1)
        for g in range(G_B):
            lg_s[g * grows:(g + 1) * grows, past:lp] = _dot_nt(
                qb[g * grows:(g + 1) * grows], _pad_rows(knew[:, g * DH:(g + 1) * DH], LANES))
        lg = _tail_bias(lg_s[...], cb_ref, tail_ref) + neg_mask
        p = jnp.exp(lg - jnp.max(lg, axis=1, keepdims=True))
        o = jnp.concatenate(
            [_dot_nt(p[g * grows:(g + 1) * grows, :past], vt_s[g * DH:(g + 1) * DH, :])
             + _dot(p[g * grows:(g + 1) * grows, past:], _pad_rows(vnew[:, g * DH:(g + 1) * DH], LANES))
             for g in range(G_B)], axis=0) / jnp.sum(p, axis=1, keepdims=True)
        o_ref[...] = _head_unstack(o, tdec)


def _mla_s_kernel(pt_ref, *refs, pp, tdec):
    pages = refs[:pp]
    ql_ref, qr_ref, rc_ref, wuv_ref, o_ref, m_s, l_s, acc_s = refs[pp:]
    s = pl.program_id(1)
    rows = H_C * tdec
    qlat, qrope = _mla_q_rows(ql_ref[...], qr_ref[...])
    scale = (D_NOPE + D_ROPE) ** -0.5

    @pl.when(s == 0)
    def _():
        m_s[...] = jnp.full(m_s.shape, -jnp.inf, F32)
        l_s[...] = jnp.zeros_like(l_s)
        acc_s[...] = jnp.zeros_like(acc_s)

    ckvt = _cat_pages(pages, 0, KV_LORA).astype(BF16)
    krt = _cat_pages(pages, KV_LORA, ROW_C).astype(BF16)
    sc = (jnp.dot(qlat, ckvt, preferred_element_type=F32) + jnp.dot(qrope, krt, preferred_element_type=F32)) * scale
    m_i, l_i, acc = _softmax_step(sc, lambda p: _dot_nt(p, ckvt), m_s[...], l_s[...], acc_s[...])
    m_s[...] = m_i
    l_s[...] = l_i
    acc_s[...] = acc

    @pl.when(s == pl.num_programs(1) - 1)
    def _():
        ckv = _pad_rows(rc_ref[:, 0:KV_LORA], LANES).astype(BF16)
        kr = _pad_rows(rc_ref[:, KV_LORA:ROW_C], LANES).astype(BF16)
        trow = jnp.concatenate([_row((tdec, LANES))] * H_C, axis=0)
        sn = (_dot_nt(qlat, ckv) + _dot_nt(qrope, kr)) * scale
        sn = jnp.where(_lane((rows, LANES)) <= trow, sn, NEG)
        _, l_f, acc_f = _softmax_step(sn, lambda p: _dot(p, ckv), m_s[...], l_s[...], acc_s[...])
        o_ref[...] = _mla_out(acc_f / l_f, wuv_ref, tdec)


def _bias_lookup(tab, bk):
    oh = jax.nn.one_hot(bk, N_BUCKETS, dtype=F32)
    return jnp.einsum('...b,bh->h...', oh, tab.astype(F32), precision=lax.Precision.HIGHEST)


def _bias_tables(tab, n_var, tq, chk, step):
    heads = tab.shape[1]
    i = jnp.arange(tq, dtype=I32)[:, None]
    j = jnp.arange(chk, dtype=I32)[None, :]
    out = []
    for r in range(n_var):
        dist = i - j + step * r
        bias = jnp.where(dist[None] < 0, NEG, _bias_lookup(tab, _t5_bucket(dist)))
        out.append(bias.reshape(heads * tq, chk))
    return jnp.stack(out)


def _tail_tables(tab, tdec):
    heads = tab.shape[1]
    t = jnp.arange(tdec, dtype=I32)[:, None]
    j = jnp.arange(2 * LANES, dtype=I32)[None, :]
    full = _bias_lookup(tab, _t5_bucket(t + LANES - j)).reshape(heads * tdec, 2 * LANES)
    far = tab[_t5_bucket(jnp.int32(1 << 20))].astype(F32)
    cb = jnp.repeat(far, tdec)[:, None]
    return cb, full - cb


def _pick_tile(n, cap, mult):
    best = mult
    for c in range(mult, cap + 1, mult):
        if n % c == 0:
            best = c
    return best


def kernel(x_prompt, x_sample, cache_a, cache_b, cache_c, state_d, page_table, p_prompt, p_sample, norm1_g, w_in, cq_norm_g, w_uq, ckv_norm_g, w_uk, w_uv, lb_logits, d_norm_g, t5_bias, w_branch, w_out, norm2_g, peer_wq, peer_subkeys, peer_u, peer_v, ple_gate, ple_proj, final_norm_g):
    bp, tp, d = x_prompt.shape
    bs, ts, _ = x_sample.shape
    depth = w_in.shape[0]
    npages = page_table.shape[1]
    page = cache_a.shape[2]
    past = npages * page
    n_p = bp * tp
    n_s = bs * ts
    n = n_p + n_s
    assert page == LANES and ts == 8 and tp % MOBA_BLOCK == 0 and past % MOBA_BLOCK == 0
    tm = _pick_tile(math.gcd(n_p, n_s), 256, 8)
    assert n_p % tm == 0 and n % LANES == 0
    tn_peer = _pick_tile(n, 640, LANES)
    tq = 128
    tq_a = 256
    chk = min(512, tp)
    pp = next(c for c in (32, 16, 8, 2) if npages % c == 0)
    assert tp % chk == 0 and chk % tq_a == 0 and tp % tq_a == 0

    x = jnp.concatenate([x_prompt.reshape(n_p, d), x_sample.reshape(n_s, d)], axis=0)
    p_all = jnp.concatenate([p_prompt.reshape(depth, n_p, -1), p_sample.reshape(depth, n_s, -1)], axis=1)

    sm = jax.nn.softmax(lb_logits.astype(F32), axis=0)
    lb_all = jnp.maximum(jnp.cumsum(sm, axis=0) - sm[0], 0.0)
    zc = lambda k: jnp.zeros((depth, d, k), F32)
    o = np.cumsum((0,) + (H_A * DH, DH, DH, H_IDX * D_IDX, H_IDX, D_IDX, H_B * DH, G_B * DH, G_B * DH,
                          Q_LORA, KV_LORA, D_ROPE, 4 * H_D * DK_D))
    seg = lambda a, b: w_in[:, :, o[a]:o[b]]
    w_cat = jnp.concatenate([
        seg(12, 13),
        seg(0, 1),
        seg(1, 3), seg(5, 6), seg(4, 5), zc(256 - ROW_A - H_IDX),
        seg(6, 7), seg(7, 9),
        seg(9, 10), zc(256 - Q_LORA),
        seg(3, 4), seg(10, 11), seg(11, 12), zc(LANES - D_ROPE)], axis=2).astype(BF16)
    w_gate = w_in[:, :, o[13]:].astype(BF16)
    cqg = jnp.pad(cq_norm_g, ((0, 0), (0, 256 - Q_LORA)))[:, None, :]
    wuq3 = w_uq.reshape(depth, Q_LORA, H_C, D_NOPE + D_ROPE)
    wuq_p = jnp.concatenate([wuq3[..., :D_NOPE].reshape(depth, Q_LORA, H_C * D_NOPE),
                             wuq3[..., D_NOPE:].reshape(depth, Q_LORA, H_C * D_ROPE)], axis=2)
    wuq_p = jnp.pad(wuq_p, ((0, 0), (0, 256 - Q_LORA), (0, 0))).astype(BF16)
    wukbd = jnp.zeros((depth, H_C, D_NOPE, H_C, KV_LORA), F32)
    wuvp = jnp.zeros((depth, H_C, KV_LORA, H_C, D_V_C), F32)
    for h in range(H_C):
        wukbd = wukbd.at[:, h, :, h, :].set(jnp.swapaxes(w_uk[:, :, h, :], 1, 2))
        wuvp = wuvp.at[:, h, :, h, :].set(w_uv[:, :, h, :])
    wukbd = wukbd.reshape(depth, H_C * D_NOPE, H_C * KV_LORA).astype(BF16)
    wuvp = wuvp.reshape(depth, H_C, KV_LORA, H_C * D_V_C).astype(BF16)
    dgt = jnp.tile(d_norm_g, (1, H_D))[:, None, :]
    hd = lax.broadcasted_iota(I32, (H_D * DK_D, H_D * DK_D), 0) // DK_D
    ones_bd = (hd == hd.T).astype(BF16)
    wb = w_branch.astype(BF16)
    wo = w_out.astype(BF16)
    wq = peer_wq.astype(BF16)
    sk = peer_subkeys.reshape(depth, 2 * PEER_HEADS, N_KEYS, -1).astype(BF16)
    u_b = peer_u.astype(BF16)
    vt_b = jnp.swapaxes(peer_v, 1, 2).astype(BF16)
    pg = ple_gate.astype(BF16)
    ppj = ple_proj.astype(BF16)

    pos = jnp.concatenate([jnp.tile(jnp.arange(tp, dtype=I32), bp),
                           jnp.tile(past + jnp.arange(ts, dtype=I32), bs)])
    half = D_ROPE // 2
    inv = 1.0 / (ROPE_THETA ** (jnp.arange(half, dtype=F32) / half))
    ang = pos.astype(F32)[:, None] * inv
    cos_t = jnp.tile(jnp.cos(ang), (1, LANES // half))
    sgn = jnp.where((jnp.arange(LANES) % D_ROPE) < half, -1.0, 1.0).astype(F32)
    sin_t = jnp.tile(jnp.sin(ang), (1, LANES // half)) * sgn
    bt_a = _bias_tables(t5_bias[:, :H_A], chk // tq_a + 2, tq_a, chk, tq_a)
    bt_b = _bias_tables(t5_bias[:, H_A:], 3, MOBA_BLOCK, MOBA_BLOCK, MOBA_BLOCK)
    cb_a, tail_a = _tail_tables(t5_bias[:, :H_A], ts)
    cb_b, tail_b = _tail_tables(t5_bias[:, H_A:], ts)
    tri_c = (lax.broadcasted_iota(I32, (chk, chk), 0) < lax.broadcasted_iota(I32, (chk, chk), 1)).astype(BF16)
    tri_l = tri_c[:LANES, :LANES]
    topk_p = min(TOPK_A_MAX, tp // 4)
    topk_s = min(TOPK_A_MAX, (past + ts) // 4)
    w = H_D * DK_D
    st0_p = jnp.zeros((bp, w, w), F32)
    lp = past + LANES
    rows_dec = H_A * ts
    npool = cache_a.shape[1]
    cache_at = jnp.swapaxes(cache_a, 2, 3)
    cache_bt = jnp.transpose(cache_b, (0, 1, 3, 4, 5, 2)).reshape(depth, npool, 2 * G_B * DH, page)
    cache_ct = jnp.swapaxes(cache_c, 2, 3)
    nbl = -(-(past // MOBA_BLOCK) // LANES) * LANES

    rows_a, rows_b, rows_c, st_p, st_s = [], [], [], [], []
    y = None
    for li in range(depth):
        p = _proj_in(x, norm1_g[li][None, :], w_cat[li], tm)
        rc, ql, qr = _mla_prep(p, cos_t, sin_t, cqg[li], ckv_norm_g[li][None, :], wuq_p[li], wukbd[li], tm)
        rows_a.append(p[:, COL_AKV:COL_AKV + ROW_A])
        rows_b.append(p[:, COL_BKV:COL_BKV + 2 * G_B * DH])
        rows_c.append(rc)

        oa_p = _dsa_prompt(p, bt_a, tri_c, bp, tp, tq_a, chk, topk_p)
        ob_p = _moba_prompt(p, bt_b, bp, tp)
        oc_p = _mla_prompt(ql, qr, rc, wuvp[li], bp, tp, tq, chk)
        lbl = lb_all[li][None, :]
        od_p, stp = _hgrn(p, lbl, dgt[li], ones_bd, st0_p, bp, tp, 0, min(tp, 256), math.gcd(tp, CHUNK_D))
        st0_s = jnp.zeros((bs, H_D, DV_D, H_D, DK_D), F32)
        for h in range(H_D):
            st0_s = st0_s.at[:, h, :, h, :].set(jnp.swapaxes(state_d[li, :, h].astype(F32), 1, 2))
        st0_s = st0_s.reshape(bs, w, w)
        od_s, sts = _hgrn(p, lbl, dgt[li], ones_bd, st0_s, bs, ts, n_p, ts, math.gcd(ts, CHUNK_D))

        oa_s = _sample_call(
            functools.partial(_dsa_s_kernel, pp=pp, past=past, topk=topk_s, tdec=ts), "dsa_sample",
            cache_at, li, page_table, pp,
            [(p, 256, COL_AQ // 256), (p, 256, COL_AKV // 256), (p, 128, COL_AIQ // 128)],
            [cb_a, tail_a, tri_l], bs, ts, n_p,
            [pltpu.VMEM((ts, lp), I32), pltpu.VMEM((rows_dec, lp), F32), pltpu.VMEM((DH, past), BF16)])
        ob_s = _sample_call(
            functools.partial(_moba_s_kernel, pp=pp, past=past,
                              ntop=min(MOBA_TOPK, past // MOBA_BLOCK + 1), tdec=ts), "moba_sample",
            cache_bt, li, page_table, pp,
            [(p, 256, COL_BQ // 256), (p, 256, COL_BKV // 256)],
            [cb_b, tail_b], bs, ts, n_p,
            [pltpu.VMEM((rows_dec, lp), F32), pltpu.VMEM((G_B * DH, past), BF16),
             pltpu.VMEM((G_B * DH, nbl), F32)])
        oc_s = _sample_call(
            functools.partial(_mla_s_kernel, pp=pp, tdec=ts), "mla_sample",
            cache_ct, li, page_table, pp,
            [(ql, H_C * KV_LORA, 0), (qr, LANES, 0), (rc, ROW_C, 0)],
            [wuvp[li]], bs, ts, n_p,
            [pltpu.VMEM((rows_dec, 1), F32), pltpu.VMEM((rows_dec, 1), F32), pltpu.VMEM((rows_dec, KV_LORA), F32)])
        st_p.append(stp)
        st_s.append(sts)

        cat = lambda a, b: jnp.concatenate([a, b], axis=0)
        x1 = _merge(x, cat(oa_p, oa_s), cat(ob_p, ob_s), cat(oc_p, oc_s), cat(od_p, od_s),
                    norm1_g[li][None, :], w_gate[li], wb[li], wo[li], tm)
        g2 = norm2_g[li][None, :]
        th, s1, e0, e1 = _peer_select(x1, g2, wq[li], sk[li], _pick_tile(n, 256, LANES))
        po = _peer_dense(x1, g2, u_b[li], vt_b[li], th, s1, e0, e1, tn_peer, 8)
        x, y = _ple(x1, po, p_all[li], pg[li], ppj[li], final_norm_g[None, :], tm)

    def unstate(st, nb):
        s5 = jnp.stack(st).reshape(depth, nb, H_D, DV_D, H_D, DK_D)
        diag = jnp.stack([s5[:, :, h, :, h, :] for h in range(H_D)], axis=2)
        return jnp.swapaxes(diag, 3, 4)

    ra = jnp.stack(rows_a)
    rb = jnp.stack(rows_b)
    rcs = jnp.stack(rows_c)
    return (y[:n_p].reshape(bp, tp, d), y[n_p:].reshape(bs, ts, d),
            ra[:, :n_p].reshape(depth, bp, tp, ROW_A),
            rb[:, :n_p].reshape(depth, bp, tp, 2, G_B, DH),
            rcs[:, :n_p].reshape(depth, bp, tp, ROW_C),
            unstate(st_p, bp),
            ra[:, n_p:].reshape(depth, bs, ts, ROW_A),
            rb[:, n_p:].reshape(depth, bs, ts, 2, G_B, DH),
            rcs[:, n_p:].reshape(depth, bs, ts, ROW_C),
            unstate(st_s, bs))
```

```python
import functools
import math

import jax
import jax.numpy as jnp
import numpy as np
from jax import lax
from jax.experimental import pallas as pl
from jax.experimental.pallas import tpu as pltpu

F32 = jnp.float32
BF16 = jnp.bfloat16
I32 = jnp.int32

DH = 64
H_A = 4
H_IDX = 4
D_IDX = 32
TOPK_A_MAX = 256
H_B = 4
G_B = 2
MOBA_BLOCK = 256
MOBA_TOPK = 3
H_C = 4
Q_LORA = 192
KV_LORA = 128
D_NOPE = 64
D_ROPE = 32
D_V_C = 64
ROPE_THETA = 10000.0
H_D = 4
DK_D = 64
DV_D = 64
CHUNK_D = 16
N_BRANCH = 4
BRANCH_W = 256
N_BUCKETS = 32
MAX_DIST = 128
N_KEYS = 128
PEER_HEADS = 8
PEER_TOPK = 16
EPS = 1e-6
NEG = -1e30
F_MIN = 1e-20
ROW_A = 2 * DH + D_IDX
ROW_C = KV_LORA + D_ROPE

LANES = 128
VMEM_LIMIT = 56 * 1024 * 1024
INT_MIN = -(2 ** 31)

COL_D = 0
COL_AQ = 1024
COL_AKV = 1280
COL_BQ = 1536
COL_BKV = 1792
COL_CQ = 2048
COL_AIQ = 2304
COL_CKV = 2432
COL_CKR = 2560
P_COLS = 2688


def _cparams(*sem):
    return pltpu.CompilerParams(dimension_semantics=sem, vmem_limit_bytes=VMEM_LIMIT)


def _dot(a, b):
    return jnp.dot(a.astype(BF16), b.astype(BF16), preferred_element_type=F32)


def _dot_nt(a, b):
    return lax.dot_general(a.astype(BF16), b.astype(BF16), (((1,), (1,)), ((), ())),
                           preferred_element_type=F32)


def _split(a):
    hi = a.astype(BF16)
    lo = (a - hi.astype(F32)).astype(BF16)
    return hi, lo


def _dot3_nt(a, b):
    ah, al = _split(a)
    bh, bl = _split(b)
    dn = (((1,), (1,)), ((), ()))
    d = lambda x, y: lax.dot_general(x, y, dn, preferred_element_type=F32)
    return d(ah, bh) + (d(ah, bl) + d(al, bh))


def _rms(x, g):
    return x * lax.rsqrt(jnp.mean(x * x, axis=-1, keepdims=True) + EPS) * g


def _t5_bucket(rel):
    n = jnp.maximum(rel, 0)
    exact = N_BUCKETS // 2
    nf = jnp.maximum(n, exact).astype(F32)
    big = exact + (jnp.log(nf / exact) / math.log(MAX_DIST / exact) * (N_BUCKETS - exact)).astype(I32)
    return jnp.where(n < exact, n, jnp.minimum(big, N_BUCKETS - 1))


def _sort_key(s):
    bits = pltpu.bitcast(s, I32)
    return jnp.where(bits < 0, bits ^ jnp.int32(0x7FFFFFFF), bits)


def _lane(shape):
    return lax.broadcasted_iota(I32, shape, 1)


def _row(shape):
    return lax.broadcasted_iota(I32, shape, 0)


def _in_kernel(x_ref, g_ref, w_ref, o_ref):
    h = _rms(x_ref[...], g_ref[...])
    o_ref[...] = _dot(h, w_ref[...])


def _proj_in(x, g, w, tm):
    n, d = x.shape
    return pl.pallas_call(
        _in_kernel,
        grid=(n // tm,),
        in_specs=[pl.BlockSpec((tm, d), lambda i: (i, 0)),
                  pl.BlockSpec((1, d), lambda i: (0, 0)),
                  pl.BlockSpec((d, P_COLS), lambda i: (0, 0))],
        out_specs=pl.BlockSpec((tm, P_COLS), lambda i: (i, 0)),
        out_shape=jax.ShapeDtypeStruct((n, P_COLS), F32),
        compiler_params=_cparams("parallel"),
        name="proj_in",
    )(x, g, w)


def _rope_lanes(x, cos, sin_s):
    lane = _lane(x.shape)
    sw = jnp.where((lane % D_ROPE) < D_ROPE // 2, pltpu.roll(x, LANES - D_ROPE // 2, 1),
                   pltpu.roll(x, D_ROPE // 2, 1))
    return x * cos + sw * sin_s


def _mla_prep_kernel(cq_ref, ckv_ref, ckr_ref, cos_ref, sin_ref, cqg_ref, ckvg_ref, wuq_ref, wuk_ref,
                     rc_ref, ql_ref, qr_ref):
    cq = cq_ref[...]
    ms = jnp.sum(cq * cq, axis=-1, keepdims=True) * (1.0 / Q_LORA)
    hq = cq * lax.rsqrt(ms + EPS) * cqg_ref[...]
    qc = _dot(hq, wuq_ref[...])
    cos = cos_ref[...]
    sin_s = sin_ref[...]
    qr_ref[...] = _rope_lanes(qc[:, H_C * D_NOPE:], cos, sin_s)
    ql_ref[...] = _dot(qc[:, :H_C * D_NOPE], wuk_ref[...])
    rc_ref[:, 0:KV_LORA] = _rms(ckv_ref[...], ckvg_ref[...])
    rc_ref[:, KV_LORA:ROW_C] = _rope_lanes(ckr_ref[...], cos, sin_s)[:, 0:D_ROPE]


def _mla_prep(p, cos_t, sin_t, cqg, ckvg, wuq, wukbd, tm):
    n = p.shape[0]
    blk = lambda w, c: pl.BlockSpec((tm, w), lambda i, c=c: (i, c))
    cst = lambda a: pl.BlockSpec(a.shape, lambda i: (0,) * a.ndim)
    return pl.pallas_call(
        _mla_prep_kernel,
        grid=(n // tm,),
        in_specs=[blk(256, COL_CQ // 256), blk(128, COL_CKV // 128), blk(128, COL_CKR // 128),
                  blk(128, 0), blk(128, 0), cst(cqg), cst(ckvg), cst(wuq), cst(wukbd)],
        out_specs=[pl.BlockSpec((tm, ROW_C), lambda i: (i, 0)),
                   pl.BlockSpec((tm, H_C * KV_LORA), lambda i: (i, 0)),
                   pl.BlockSpec((tm, LANES), lambda i: (i, 0))],
        out_shape=[jax.ShapeDtypeStruct((n, ROW_C), F32),
                   jax.ShapeDtypeStruct((n, H_C * KV_LORA), F32),
                   jax.ShapeDtypeStruct((n, LANES), F32)],
        compiler_params=_cparams("parallel"),
        name="mla_prep",
    )(p, p, p, cos_t, sin_t, cqg, ckvg, wuq, wukbd)


def _merge_kernel(x_ref, oa_ref, ob_ref, oc_ref, od_ref, g_ref, wg_ref, wb_ref, wo_ref, o_ref):
    x = x_ref[...]
    h = _rms(x, g_ref[...]).astype(BF16)
    d = x.shape[1]
    mix = None
    for m, o_m in enumerate((oa_ref, ob_ref, oc_ref, od_ref)):
        gate = jax.nn.sigmoid(jnp.dot(h, wg_ref[:, m * d:(m + 1) * d], preferred_element_type=F32))
        term = gate * _dot(o_m[...], wb_ref[m])
        mix = term if mix is None else mix + term
    o_ref[...] = x + _dot(mix, wo_ref[...])


def _merge(x, oa, ob, oc, od, g, wg, wb, wo, tm):
    n, d = x.shape
    row = lambda w: pl.BlockSpec((tm, w), lambda i: (i, 0))
    cst = lambda a: pl.BlockSpec(a.shape, lambda i: (0,) * a.ndim)
    return pl.pallas_call(
        _merge_kernel,
        grid=(n // tm,),
        in_specs=[row(d), row(BRANCH_W), row(BRANCH_W), row(BRANCH_W), row(BRANCH_W),
                  cst(g), cst(wg), cst(wb), cst(wo)],
        out_specs=row(d),
        out_shape=jax.ShapeDtypeStruct((n, d), F32),
        compiler_params=_cparams("parallel"),
        name="merge",
    )(x, oa, ob, oc, od, g, wg, wb, wo)


def _ple_kernel(x_ref, po_ref, p_ref, pg_ref, pp_ref, gf_ref, o_ref, y_ref):
    x2 = x_ref[...] + po_ref[...]
    xn = x2 + jax.nn.sigmoid(_dot(x2, pg_ref[...])) * _dot(p_ref[...], pp_ref[...])
    o_ref[...] = xn
    y_ref[...] = _rms(xn, gf_ref[...])


def _ple(x, po, p, pg, pp, gf, tm):
    n, d = x.shape
    row = lambda w: pl.BlockSpec((tm, w), lambda i: (i, 0))
    cst = lambda a: pl.BlockSpec(a.shape, lambda i: (0,) * a.ndim)
    return pl.pallas_call(
        _ple_kernel,
        grid=(n // tm,),
        in_specs=[row(d), row(d), row(p.shape[1]), cst(pg), cst(pp), cst(gf)],
        out_specs=[row(d), row(d)],
        out_shape=[jax.ShapeDtypeStruct((n, d), F32), jax.ShapeDtypeStruct((n, d), F32)],
        compiler_params=_cparams("parallel"),
        name="ple",
    )(x, po, p, pg, pp, gf)


PEER_NCAND = 80
PEER_NTOP = 24


def _extract_top(w_s, out_s, n_chains, count):
    def body(r, carry):
        for c in range(n_chains):
            w = w_s[c]
            m = jnp.max(w, axis=0, keepdims=True)
            out_s[c, pl.ds(r, 1), :] = m
            w_s[c] = jnp.where(w == m, -jnp.inf, w)
        return carry
    lax.fori_loop(0, count, body, 0)


def _peer_sel_kernel(x_ref, g_ref, wq_ref, sk_ref, th_ref, s1_ref, e0_ref, e1_ref, w_s, sv_s, c_s, top_s):
    hn = _rms(x_ref[...], g_ref[...])
    q = _dot(hn, wq_ref[...])
    for hp in range(2 * PEER_HEADS):
        s = _dot_nt(sk_ref[hp], q[:, hp * N_KEYS:(hp + 1) * N_KEYS])
        w_s[hp] = s
        if hp % 2 == 0:
            th_ref[hp // 2] = s
        else:
            s1_ref[hp // 2] = s
    _extract_top(w_s, sv_s, 2 * PEER_HEADS, PEER_TOPK)
    half = PEER_TOPK // 2
    for h in range(PEER_HEADS):
        sv0 = sv_s[2 * h]
        sv1 = sv_s[2 * h + 1]
        pieces = [sv0[0:1, :] + sv1]
        pieces += [sv0[a:a + 1, :] + sv1[0:half, :] for a in range(1, half)]
        pieces.append(sv0[half:PEER_TOPK, :] + sv1[0:1, :])
        c_s[h] = jnp.concatenate(pieces, axis=0)
    _extract_top(c_s, top_s, PEER_HEADS, PEER_TOPK + 1)
    for h in range(PEER_HEADS):
        top = top_s[h]
        z = jnp.sum(jnp.exp(top[0:PEER_TOPK, :] - top[0:1, :]), axis=0, keepdims=True)
        cut = 0.5 * (top[PEER_TOPK - 1:PEER_TOPK, :] + top[PEER_TOPK:PEER_TOPK + 1, :])
        s0 = th_ref[h]
        e0_ref[h] = jnp.exp(s0 - sv_s[2 * h, 0:1, :]) / z
        e1_ref[h] = jnp.exp(s1_ref[h] - sv_s[2 * h + 1, 0:1, :])
        th_ref[h] = cut - s0


def _peer_select(x, g, wq, sk, tn):
    n, d = x.shape
    hk = jax.ShapeDtypeStruct((PEER_HEADS, N_KEYS, n), F32)
    hk_spec = pl.BlockSpec((PEER_HEADS, N_KEYS, tn), lambda i: (0, 0, i))
    cst = lambda a: pl.BlockSpec(a.shape, lambda i: (0,) * a.ndim)
    return pl.pallas_call(
        _peer_sel_kernel,
        grid=(n // tn,),
        in_specs=[pl.BlockSpec((tn, d), lambda i: (i, 0)), cst(g), cst(wq), cst(sk)],
        out_specs=[hk_spec, hk_spec, hk_spec, hk_spec],
        out_shape=[hk, hk, hk, hk],
        scratch_shapes=[pltpu.VMEM((2 * PEER_HEADS, N_KEYS, tn), F32),
                        pltpu.VMEM((2 * PEER_HEADS, PEER_TOPK, tn), F32),
                        pltpu.VMEM((PEER_HEADS, PEER_NCAND, tn), F32),
                        pltpu.VMEM((PEER_HEADS, PEER_NTOP, tn), F32)],
        compiler_params=_cparams("parallel"),
        name="peer_select",
    )(x, g, wq, sk)


def _peer_dense_kernel(x_ref, g_ref, u_ref, vt_ref, th_ref, s1_ref, e0_ref, e1_ref, o_ref, hn_s, acc_s,
                       act_s, z_s, *, ic):
    c = pl.program_id(1)
    tn = hn_s.shape[1]

    @pl.when(c == 0)
    def _():
        hn_s[...] = _rms(x_ref[...], g_ref[...]).T.astype(BF16)
        acc_s[...] = jnp.zeros_like(acc_s)
        act_s[...] = jnp.zeros_like(act_s)
        z_s[...] = jnp.zeros_like(z_s)

    cur = c % 2
    prv = 1 - cur
    for ii in range(ic):
        if ii == 1:
            act_s[cur] = jnp.dot(u_ref[...], hn_s[...], preferred_element_type=F32)
        if ii == ic // 2:
            acc_s[...] += jnp.dot(vt_ref[...], z_s[cur], preferred_element_type=F32)
        er = slice(ii * N_KEYS, (ii + 1) * N_KEYS)
        for lt in range(tn // LANES):
            tok = slice(lt * LANES, (lt + 1) * LANES)
            wt = None
            for h in range(PEER_HEADS):
                t = jnp.where(s1_ref[h, :, tok] >= th_ref[h, ii:ii + 1, tok], e1_ref[h, :, tok], 0.0)
                term = e0_ref[h, ii:ii + 1, tok] * t
                wt = term if wt is None else wt + term
            z_s[prv, er, tok] = (wt * jax.nn.gelu(act_s[prv, er, tok])).astype(BF16)

    @pl.when(c == pl.num_programs(1) - 1)
    def _():
        o_ref[...] = acc_s[...].T


def _peer_dense(x, g, u, vt, th, s1, e0, e1, tn, ic):
    n, d = x.shape
    ne = u.shape[0]
    ec = ic * N_KEYS
    nc = ne // ec
    clamp = lambda c: jnp.minimum(jnp.maximum(c, 0), nc - 1)
    return pl.pallas_call(
        functools.partial(_peer_dense_kernel, ic=ic),
        grid=(n // tn, nc + 2),
        in_specs=[pl.BlockSpec((tn, d), lambda i, c: (i, 0)),
                  pl.BlockSpec((1, d), lambda i, c: (0, 0)),
                  pl.BlockSpec((ec, d), lambda i, c: (clamp(c), 0)),
                  pl.BlockSpec((d, ec), lambda i, c: (0, clamp(c - 2))),
                  pl.BlockSpec((PEER_HEADS, ic, tn), lambda i, c: (0, clamp(c - 1), i)),
                  pl.BlockSpec((PEER_HEADS, N_KEYS, tn), lambda i, c: (0, 0, i)),
                  pl.BlockSpec((PEER_HEADS, ic, tn), lambda i, c: (0, clamp(c - 1), i)),
                  pl.BlockSpec((PEER_HEADS, N_KEYS, tn), lambda i, c: (0, 0, i))],
        out_specs=pl.BlockSpec((tn, d), lambda i, c: (i, 0)),
        out_shape=jax.ShapeDtypeStruct((n, d), F32),
        scratch_shapes=[pltpu.VMEM((d, tn), BF16), pltpu.VMEM((d, tn), F32),
                        pltpu.VMEM((2, ec, tn), F32), pltpu.VMEM((2, ec, tn), BF16)],
        compiler_params=_cparams("parallel", "arbitrary"),
        name="peer_dense",
    )(x, g, u, vt, th, s1, e0, e1)


def _hgrn_kernel(p_ref, lb_ref, dg_ref, ones_ref, st0_ref, o_ref, st_ref, q_s, k_s, b_s, v_s, o_s,
                 *, ts, ch):
    s = pl.program_id(1)
    w = H_D * DK_D

    @pl.when(s == 0)
    def _():
        st_ref[...] = st0_ref[...]

    dq = p_ref[:, 0:w]
    df = p_ref[:, w:2 * w]
    lb = lb_ref[...]
    sig = jax.nn.sigmoid(df)
    logf = jnp.log(jnp.maximum(lb + (1.0 - lb) * sig, F_MIN))
    r = _row((ts, w)) % ch
    b = logf
    sh = 1
    while sh < ch:
        b = b + jnp.where(r >= sh, pltpu.roll(b, sh, 0), 0.0)
        sh *= 2
    q_s[...] = dq * jax.nn.sigmoid(dq)
    k_s[...] = (1.0 - lb) * (1.0 - sig)
    b_s[...] = b
    v_s[...] = p_ref[:, 2 * w:3 * w]
    ones_bd = ones_ref[...]
    bd_mask = ones_bd.astype(F32)
    srow = _row((ch, w))

    def body(c, carry):
        off = pl.multiple_of(c * ch, ch)
        q = q_s[pl.ds(off, ch), :]
        kk = k_s[pl.ds(off, ch), :]
        bb = b_s[pl.ds(off, ch), :]
        v = v_s[pl.ds(off, ch), :]
        st = st_ref[0]
        o_inter = _dot_nt(q * jnp.exp(bb), st)
        rows = []
        for t in range(ch):
            dec = jnp.exp(jnp.where(srow <= t, bb[t:t + 1, :] - bb, 0.0))
            rows.append(jnp.where(srow <= t, dec * kk * q[t:t + 1, :], 0.0))
        pm = jnp.concatenate(rows, axis=0)
        rr = jnp.dot(pm.astype(BF16), ones_bd, preferred_element_type=F32)
        o_intra = jnp.sum(rr.reshape(ch, ch, w) * v[None, :, :], axis=1)
        o_s[pl.ds(off, ch), :] = o_inter + o_intra
        bl = bb[ch - 1:ch, :]
        kdec = kk * jnp.exp(bl - bb)
        upd = lax.dot_general(v.astype(BF16), kdec.astype(BF16), (((0,), (0,)), ((), ())),
                              preferred_element_type=F32)
        st_ref[0] = (st * jnp.exp(bl) + upd) * bd_mask
        return carry

    lax.fori_loop(0, ts // ch, body, 0)
    o = o_s[...]
    ms = jnp.dot((o * o).astype(BF16), ones_bd, preferred_element_type=F32) * (1.0 / DV_D)
    dg = p_ref[:, 3 * w:4 * w]
    o_ref[...] = o * lax.rsqrt(ms + EPS) * dg_ref[...] * (dg * jax.nn.sigmoid(dg))


def _hgrn(p, lb, dg, ones_bd, st0, nb, t, row0, ts, ch):
    w = H_D * DK_D
    ns = t // ts
    rb0 = row0 // ts
    cst = lambda a: pl.BlockSpec(a.shape, lambda b, s: (0,) * a.ndim)
    return pl.pallas_call(
        functools.partial(_hgrn_kernel, ts=ts, ch=ch),
        grid=(nb, ns),
        in_specs=[pl.BlockSpec((ts, 4 * w), lambda b, s: (rb0 + b * ns + s, COL_D // (4 * w))),
                  cst(lb), cst(dg), cst(ones_bd),
                  pl.BlockSpec((1, w, w), lambda b, s: (b, 0, 0))],
        out_specs=[pl.BlockSpec((ts, w), lambda b, s: (b * ns + s, 0)),
                   pl.BlockSpec((1, w, w), lambda b, s: (b, 0, 0))],
        out_shape=[jax.ShapeDtypeStruct((nb * t, w), F32), jax.ShapeDtypeStruct((nb, w, w), F32)],
        scratch_shapes=[pltpu.VMEM((ts, w), F32)] * 5,
        compiler_params=_cparams("parallel", "arbitrary"),
        name="hgrn",
    )(p, lb, dg, ones_bd, st0)


def _radix_threshold(count_ge, k, rows):
    def body(i, t):
        cand = t ^ jnp.left_shift(jnp.int32(1), 31 - i)
        return jnp.where(count_ge(cand) >= k, cand, t)
    return lax.fori_loop(0, 32, body, jnp.full((rows, 1), INT_MIN, I32))


def _head_rows_low(q, scale):
    lane = _lane((q.shape[0], LANES))
    out = []
    for m in range(2):
        tile = q[:, m * LANES:(m + 1) * LANES] * scale
        out.append(jnp.where(lane < DH, tile, 0.0))
        out.append(jnp.where(lane < DH, pltpu.roll(tile, DH, 1), 0.0))
    return jnp.concatenate(out, axis=0)


def _head_rows_group(q, scale):
    lane = _lane((q.shape[0], LANES))
    t0 = q[:, 0:LANES] * scale
    t1 = q[:, LANES:2 * LANES] * scale
    return jnp.concatenate([
        jnp.where(lane < DH, t0, 0.0),
        jnp.where(lane < DH, pltpu.roll(t0, DH, 1), 0.0),
        jnp.where(lane >= DH, pltpu.roll(t1, DH, 1), 0.0),
        jnp.where(lane >= DH, t1, 0.0)], axis=0)


def _idx_heads(iq):
    lane = _lane(iq.shape)
    return [jnp.where(lane < D_IDX, iq if h == 0 else pltpu.roll(iq, LANES - D_IDX * h, 1), 0.0)
            for h in range(H_IDX)]


def _index_score(iqh, iw, ik_tile):
    s = None
    for h in range(H_IDX):
        r = jnp.maximum(_dot3_nt(iqh[h], ik_tile), 0.0)
        term = iw[:, h:h + 1] * r
        s = term if s is None else s + term
    return s * ((H_IDX * D_IDX) ** -0.5) + 0.0


def _softmax_step(lg, pv, m_i, l_i, acc):
    m_new = jnp.maximum(m_i, jnp.max(lg, axis=1, keepdims=True))
    alpha = jnp.exp(m_i - m_new)
    p = jnp.exp(lg - m_new)
    l_new = alpha * l_i + jnp.sum(p, axis=1, keepdims=True)
    return m_new, l_new, alpha * acc + pv(p)


def _dsa_p_kernel(q_ref, qkv_ref, iq_ref, kv_ref, bt_ref, tri_ref, o_ref, key_s, *, tq, chk, topk):
    qi = pl.program_id(1)
    nch = (qi * tq + tq + chk - 1) // chk
    qpos = qi * tq + _row((tq, 1))
    iw = qkv_ref[:, ROW_A:ROW_A + H_IDX]
    iqh = _idx_heads(iq_ref[...])
    nv = bt_ref.shape[0]

    def kpos_of(off):
        return off + _lane((tq, chk))

    def score_body(c, carry):
        off = pl.multiple_of(c * chk, chk)
        s = _index_score(iqh, iw, kv_ref[pl.ds(off, chk), LANES:2 * LANES])
        s = jnp.where(kpos_of(off) <= qpos, s, NEG)
        key_s[:, pl.ds(off, chk)] = _sort_key(s)
        return carry

    lax.fori_loop(0, nch, score_body, 0)

    rblk = 64

    def count(pred, ref_val):
        def body(c, acc):
            off = pl.multiple_of(c * chk, chk)
            out = []
            for rb in range(tq // rblk):
                rows = slice(rb * rblk, (rb + 1) * rblk)
                a = acc[rows]
                rv = ref_val[rows]
                for l in range(chk // LANES):
                    kc = key_s[rows, pl.ds(pl.multiple_of(off + l * LANES, LANES), LANES)]
                    a = a + pred(kc, rv).astype(I32)
                out.append(a)
            return jnp.concatenate(out, axis=0)
        acc = lax.fori_loop(0, nch, body, jnp.zeros((tq, LANES), I32))
        return jnp.sum(acc, axis=1, keepdims=True)

    thr = _radix_threshold(lambda cand: count(lambda kc, cd: kc >= cd, cand), topk, tq)
    need = (topk - count(lambda kc, t: kc > t, thr)).astype(F32)

    qs = _head_rows_low(q_ref[...], DH ** -0.5).astype(BF16)

    def att_body(c, carry):
        m_i, l_i, acc, run = carry
        off = pl.multiple_of(c * chk, chk)
        kc = key_s[:, pl.ds(off, chk)]
        eq = kc == thr
        eqf = jnp.where(eq, 1.0, 0.0)
        pref = jnp.dot(eqf.astype(BF16), tri_ref[...], preferred_element_type=F32) + run
        sel = (kc > thr) | (eq & (pref < need))
        kvt = kv_ref[pl.ds(off, chk), 0:LANES].astype(BF16)
        lg = lax.dot_general(qs, kvt, (((1,), (1,)), ((), ())), preferred_element_type=F32)
        r = jnp.minimum(qi - c * (chk // tq), nv - 1)
        lg = lg + bt_ref[r]
        lg = lg + jnp.concatenate([jnp.where(sel, 0.0, NEG)] * H_A, axis=0)
        m_i, l_i, acc = _softmax_step(lg, lambda p: _dot(p, kvt), m_i, l_i, acc)
        return m_i, l_i, acc, run + jnp.sum(eqf, axis=1, keepdims=True)

    init = (jnp.full((H_A * tq, 1), -jnp.inf, F32), jnp.zeros((H_A * tq, 1), F32),
            jnp.zeros((H_A * tq, LANES), F32), jnp.zeros((tq, 1), F32))
    _, l_i, acc, _ = lax.fori_loop(0, nch, att_body, init)
    o = acc / l_i
    lane = _lane((tq, LANES))
    for m in range(2):
        o_ref[:, m * LANES:(m + 1) * LANES] = jnp.where(
            lane < DH, pltpu.roll(o[2 * m * tq:(2 * m + 1) * tq], DH, 1), o[(2 * m + 1) * tq:(2 * m + 2) * tq])


def _dsa_prompt(p, bt, tri, nb, t, tq, chk, topk):
    nq = t // tq
    return pl.pallas_call(
        functools.partial(_dsa_p_kernel, tq=tq, chk=chk, topk=topk),
        grid=(nb, nq),
        in_specs=[pl.BlockSpec((tq, 256), lambda b, i: (b * nq + i, COL_AQ // 256)),
                  pl.BlockSpec((tq, 256), lambda b, i: (b * nq + i, COL_AKV // 256)),
                  pl.BlockSpec((tq, 128), lambda b, i: (b * nq + i, COL_AIQ // 128)),
                  pl.BlockSpec((t, 256), lambda b, i: (b, COL_AKV // 256)),
                  pl.BlockSpec(bt.shape, lambda b, i: (0, 0, 0)),
                  pl.BlockSpec(tri.shape, lambda b, i: (0, 0))],
        out_specs=pl.BlockSpec((tq, BRANCH_W), lambda b, i: (b * nq + i, 0)),
        out_shape=jax.ShapeDtypeStruct((nb * t, BRANCH_W), F32),
        scratch_shapes=[pltpu.VMEM((tq, t), I32)],
        compiler_params=_cparams("parallel", "arbitrary"),
        name="dsa_prompt",
    )(p, p, p, p, bt, tri)


def _moba_p_kernel(q_ref, kv_ref, bt_ref, o_ref, m_s, l_s, acc_s, *, nblk, ntop):
    qi = pl.program_id(1)
    tq = MOBA_BLOCK
    rows = H_B * tq
    qs = _head_rows_group(q_ref[...], DH ** -0.5)
    kmean = jnp.sum(kv_ref[:, 0:LANES].reshape(nblk, MOBA_BLOCK, LANES), axis=1) * (1.0 / MOBA_BLOCK)
    gate = _dot3_nt(qs, kmean)
    n_l = _lane((rows, nblk))
    gate = jnp.where(n_l < qi, gate, NEG)
    rank = jnp.zeros((rows, nblk), I32)
    for m in range(nblk):
        col = gate[:, m:m + 1]
        beats = (col > gate) | ((col == gate) & (m < n_l))
        rank = rank + beats.astype(I32)
    bm = ((rank < ntop) & (gate > 0.5 * NEG)) | (n_l == qi)
    bneg = jnp.where(bm, 0.0, NEG)
    m_s[...] = jnp.full(m_s.shape, -jnp.inf, F32)
    l_s[...] = jnp.zeros(l_s.shape, F32)
    acc_s[...] = jnp.zeros(acc_s.shape, F32)
    qb = qs.astype(BF16)
    for n in range(nblk):
        @pl.when(n <= qi)
        def _(n=n):
            kt = kv_ref[n * MOBA_BLOCK:(n + 1) * MOBA_BLOCK, 0:LANES].astype(BF16)
            vt = kv_ref[n * MOBA_BLOCK:(n + 1) * MOBA_BLOCK, LANES:2 * LANES].astype(BF16)
            lg = lax.dot_general(qb, kt, (((1,), (1,)), ((), ())), preferred_element_type=F32)
            lg = lg + bt_ref[jnp.minimum(qi - n, bt_ref.shape[0] - 1)] + bneg[:, n:n + 1]
            m_i, l_i, acc = _softmax_step(lg, lambda p: _dot(p, vt), m_s[...], l_s[...], acc_s[...])
            m_s[...] = m_i
            l_s[...] = l_i
            acc_s[...] = acc
    o = acc_s[...] / l_s[...]
    lane = _lane((tq, LANES))
    o_ref[:, 0:LANES] = jnp.where(lane < DH, o[0:tq], pltpu.roll(o[tq:2 * tq], DH, 1))
    o_ref[:, LANES:2 * LANES] = jnp.where(lane < DH, pltpu.roll(o[2 * tq:3 * tq], DH, 1), o[3 * tq:4 * tq])


def _moba_prompt(p, bt, nb, t):
    tq = MOBA_BLOCK
    nq = t // tq
    rows = H_B * tq
    return pl.pallas_call(
        functools.partial(_moba_p_kernel, nblk=nq, ntop=min(MOBA_TOPK, nq)),
        grid=(nb, nq),
        in_specs=[pl.BlockSpec((tq, 256), lambda b, i: (b * nq + i, COL_BQ // 256)),
                  pl.BlockSpec((t, 256), lambda b, i: (b, COL_BKV // 256)),
                  pl.BlockSpec(bt.shape, lambda b, i: (0, 0, 0))],
        out_specs=pl.BlockSpec((tq, BRANCH_W), lambda b, i: (b * nq + i, 0)),
        out_shape=jax.ShapeDtypeStruct((nb * t, BRANCH_W), F32),
        scratch_shapes=[pltpu.VMEM((rows, 1), F32), pltpu.VMEM((rows, 1), F32), pltpu.VMEM((rows, LANES), F32)],
        compiler_params=_cparams("parallel", "arbitrary"),
        name="moba_prompt",
    )(p, p, bt)


def _mla_q_rows(ql, qr):
    qlat = jnp.concatenate([ql[:, h * KV_LORA:(h + 1) * KV_LORA] for h in range(H_C)], axis=0)
    qrope = jnp.concatenate([qr[:, h * D_ROPE:(h + 1) * D_ROPE] for h in range(H_C)], axis=0)
    return qlat.astype(BF16), qrope.astype(BF16)


def _mla_out(o_lat, wuv_ref, t):
    out = None
    for h in range(H_C):
        term = _dot(o_lat[h * t:(h + 1) * t], wuv_ref[h])
        out = term if out is None else out + term
    return out


def _mla_p_kernel(ql_ref, qr_ref, rc_ref, wuv_ref, o_ref, *, tq, chk):
    qi = pl.program_id(1)
    nch = (qi * tq + tq + chk - 1) // chk
    rows = H_C * tq
    qlat, qrope = _mla_q_rows(ql_ref[...], qr_ref[...])
    qpos = qi * tq + _row((tq, 1))
    qpos4 = jnp.concatenate([qpos] * H_C, axis=0)
    scale = (D_NOPE + D_ROPE) ** -0.5
    dn = (((1,), (1,)), ((), ()))

    def body(c, carry):
        m_i, l_i, acc = carry
        off = pl.multiple_of(c * chk, chk)
        ckv = rc_ref[pl.ds(off, chk), 0:KV_LORA].astype(BF16)
        kr = rc_ref[pl.ds(off, chk), KV_LORA:ROW_C].astype(BF16)
        s = (lax.dot_general(qlat, ckv, dn, preferred_element_type=F32)
             + lax.dot_general(qrope, kr, dn, preferred_element_type=F32)) * scale
        kpos = off + _lane((rows, chk))
        s = jnp.where(kpos <= qpos4, s, NEG)
        return _softmax_step(s, lambda p: _dot(p, ckv), m_i, l_i, acc)

    init = (jnp.full((rows, 1), -jnp.inf, F32), jnp.zeros((rows, 1), F32), jnp.zeros((rows, KV_LORA), F32))
    _, l_i, acc = lax.fori_loop(0, nch, body, init)
    o_ref[...] = _mla_out(acc / l_i, wuv_ref, tq)


def _mla_prompt(ql, qr, rc, wuvp, nb, t, tq, chk):
    nq = t // tq
    return pl.pallas_call(
        functools.partial(_mla_p_kernel, tq=tq, chk=chk),
        grid=(nb, nq),
        in_specs=[pl.BlockSpec((tq, H_C * KV_LORA), lambda b, i: (b * nq + i, 0)),
                  pl.BlockSpec((tq, LANES), lambda b, i: (b * nq + i, 0)),
                  pl.BlockSpec((t, ROW_C), lambda b, i: (b, 0)),
                  pl.BlockSpec(wuvp.shape, lambda b, i: (0, 0, 0))],
        out_specs=pl.BlockSpec((tq, BRANCH_W), lambda b, i: (b * nq + i, 0)),
        out_shape=jax.ShapeDtypeStruct((nb * t, BRANCH_W), F32),
        compiler_params=_cparams("parallel", "arbitrary"),
        name="mla_prompt",
    )(ql, qr, rc, wuvp)


def _page_specs(shape_tail, li, pp):
    nd = len(shape_tail)
    return [pl.BlockSpec((None, None) + shape_tail,
                         lambda b, s, pt, j=j: (li, pt[b, s * pp + j]) + (0,) * nd)
            for j in range(pp)]


def _cat_pages(pages, lo, hi):
    return jnp.concatenate([pg[lo:hi, :] for pg in pages], axis=1)


def _pad_rows(x, rows):
    return jnp.concatenate([x, jnp.zeros((rows - x.shape[0], x.shape[1]), x.dtype)], axis=0)


def _head_stack(x, width):
    return jnp.concatenate([x[:, h * width:(h + 1) * width] for h in range(x.shape[1] // width)], axis=0)


def _head_unstack(o, t):
    return jnp.concatenate([o[h * t:(h + 1) * t] for h in range(o.shape[0] // t)], axis=1)


def _dot3(a, b):
    ah, al = _split(a)
    bh, bl = _split(b)
    d = lambda x, y: jnp.dot(x, y, preferred_element_type=F32)
    return d(ah, bh) + (d(ah, bl) + d(al, bh))


def _tail_bias(lg, cb_ref, tail_ref):
    lg = lg + cb_ref[...]
    w = lg.shape[1] - 2 * LANES
    return jnp.concatenate([lg[:, :w], lg[:, w:] + tail_ref[...]], axis=1)


def _dsa_s_kernel(pt_ref, *refs, pp, past, topk, tdec):
    pages = refs[:pp]
    q_ref, qkv_ref, iq_ref, cb_ref, tail_ref, tri_ref, o_ref, key_s, lg_s, vt_s = refs[pp:]
    s = pl.program_id(1)
    lp = past + LANES
    wid = pp * LANES
    iw = qkv_ref[:, ROW_A:ROW_A + H_IDX]
    iqs = _head_stack(iq_ref[...], D_IDX)
    qs = (_head_stack(q_ref[...], DH) * DH ** -0.5).astype(BF16)

    def combine(r):
        sc = None
        for h in range(H_IDX):
            term = iw[:, h:h + 1] * r[h * tdec:(h + 1) * tdec]
            sc = term if sc is None else sc + term
        return sc * ((H_IDX * D_IDX) ** -0.5) + 0.0

    off = pl.multiple_of(s * wid, wid)
    ikt = _cat_pages(pages, 2 * DH, ROW_A)
    key_s[:, pl.ds(off, wid)] = _sort_key(combine(jnp.maximum(_dot3(iqs, ikt), 0.0)))
    lg_s[:, pl.ds(off, wid)] = jnp.dot(qs, _cat_pages(pages, 0, DH).astype(BF16), preferred_element_type=F32)
    vt_s[:, pl.ds(off, wid)] = _cat_pages(pages, DH, 2 * DH).astype(BF16)

    @pl.when(s == pl.num_programs(1) - 1)
    def _():
        knew = _pad_rows(qkv_ref[:, 0:DH], LANES)
        vnew = _pad_rows(qkv_ref[:, DH:2 * DH], LANES)
        iknew = _pad_rows(qkv_ref[:, 2 * DH:ROW_A], LANES)
        sc = combine(jnp.maximum(_dot3_nt(iqs, iknew), 0.0))
        causal = _lane((tdec, LANES)) <= _row((tdec, LANES))
        key_s[:, past:lp] = _sort_key(jnp.where(causal, sc, NEG))
        lg_s[:, past:lp] = _dot_nt(qs, knew)
        keys = key_s[...]
        thr = _radix_threshold(
            lambda cand: jnp.sum((keys >= cand).astype(I32), axis=1, keepdims=True), topk, tdec)
        need = (topk - jnp.sum((keys > thr).astype(I32), axis=1, keepdims=True)).astype(F32)
        eq = keys == thr
        eqf = jnp.where(eq, 1.0, 0.0)
        nck = lp // LANES
        stacked = jnp.concatenate([eqf[:, c * LANES:(c + 1) * LANES] for c in range(nck)],
                                  axis=0).astype(BF16)
        pin = jnp.dot(stacked, tri_ref[...], preferred_element_type=F32)
        tot = jnp.dot(stacked, jnp.ones((LANES, LANES), BF16), preferred_element_type=F32)
        run = jnp.zeros((tdec, LANES), F32)
        pref = []
        for c in range(nck):
            pref.append(pin[c * tdec:(c + 1) * tdec] + run)
            run = run + tot[c * tdec:(c + 1) * tdec]
        pref = jnp.concatenate(pref, axis=1)
        kpos = _lane((tdec, lp))
        sel = ((keys > thr) | (eq & (pref < need))) & ((kpos < past) | (kpos - past <= _row((tdec, lp))))
        lg = _tail_bias(lg_s[...], cb_ref, tail_ref)
        lg = lg + jnp.concatenate([jnp.where(sel, 0.0, NEG)] * H_A, axis=0)
        p = jnp.exp(lg - jnp.max(lg, axis=1, keepdims=True))
        o = (_dot_nt(p[:, :past], vt_s[...]) + _dot(p[:, past:], vnew)) / jnp.sum(p, axis=1, keepdims=True)
        o_ref[...] = _head_unstack(o, tdec)


def _sample_call(kernel, name, cache, li, pt, pp, row_inputs, const_inputs, nb, tdec, row0, scratch):
    npages = pt.shape[1]
    rb0 = row0 // tdec
    in_specs = _page_specs(cache.shape[2:], li, pp)
    args = [cache] * pp
    for arr, width, col in row_inputs:
        in_specs.append(pl.BlockSpec((tdec, width), lambda b, s, pt, col=col: (rb0 + b, col)))
        args.append(arr)
    for arr in const_inputs:
        in_specs.append(pl.BlockSpec(arr.shape, lambda b, s, pt, nd=arr.ndim: (0,) * nd))
        args.append(arr)
    return pl.pallas_call(
        kernel,
        grid_spec=pltpu.PrefetchScalarGridSpec(
            num_scalar_prefetch=1,
            grid=(nb, npages // pp),
            in_specs=in_specs,
            out_specs=pl.BlockSpec((tdec, BRANCH_W), lambda b, s, pt: (b, 0)),
            scratch_shapes=scratch),
        out_shape=jax.ShapeDtypeStruct((nb * tdec, BRANCH_W), F32),
        compiler_params=_cparams("parallel", "arbitrary"),
        name=name,
    )(pt, *args)


def _moba_s_kernel(pt_ref, *refs, pp, past, ntop, tdec):
    pages = refs[:pp]
    q_ref, kv_ref, cb_ref, tail_ref, o_ref, lg_s, vt_s, ks_s = refs[pp:]
    s = pl.program_id(1)
    lp = past + LANES
    wid = pp * LANES
    nbk = past // MOBA_BLOCK
    ppb = MOBA_BLOCK // LANES
    gw = G_B * DH
    grows = (H_B // G_B) * tdec
    rows = H_B * tdec
    qf = _head_stack(q_ref[...], DH) * DH ** -0.5
    qb = qf.astype(BF16)

    @pl.when(s == 0)
    def _():
        ks_s[...] = jnp.zeros_like(ks_s)

    off = pl.multiple_of(s * wid, wid)
    for g in range(G_B):
        ktg = _cat_pages(pages, g * DH, (g + 1) * DH).astype(BF16)
        lg_s[g * grows:(g + 1) * grows, pl.ds(off, wid)] = jnp.dot(
            qb[g * grows:(g + 1) * grows], ktg, preferred_element_type=F32)
    vt_s[:, pl.ds(off, wid)] = _cat_pages(pages, gw, 2 * gw).astype(BF16)
    blane = _lane(ks_s.shape)
    ksum = ks_s[...]
    for jb in range(pp // ppb):
        kt = pages[jb * ppb][0:gw, :]
        for j in range(1, ppb):
            kt = kt + pages[jb * ppb + j][0:gw, :]
        ksum = ksum + jnp.where(blane == s * (pp // ppb) + jb, jnp.sum(kt, axis=1, keepdims=True), 0.0)
    ks_s[...] = ksum

    @pl.when(s == pl.num_programs(1) - 1)
    def _():
        knew = kv_ref[:, 0:gw]
        vnew = kv_ref[:, gw:2 * gw]
        kmean = ks_s[...] * (1.0 / MOBA_BLOCK)
        gate = jnp.concatenate([_dot3(qf[g * grows:(g + 1) * grows], kmean[g * DH:(g + 1) * DH])
                                for g in range(G_B)], axis=0)
        n_l = _lane(gate.shape)
        gate = jnp.where(n_l < nbk, gate, NEG)
        rank = jnp.zeros(gate.shape, I32)
        for m in range(nbk):
            col = gate[:, m:m + 1]
            rank = rank + ((col > gate) | ((col == gate) & (m < n_l))).astype(I32)
        bneg = jnp.where((rank < ntop) & (gate > 0.5 * NEG), 0.0, NEG)
        trow = jnp.concatenate([_row((tdec, LANES))] * H_B, axis=0)
        neg_mask = jnp.concatenate(
            [jnp.broadcast_to(bneg[:, n:n + 1], (rows, MOBA_BLOCK)) for n in range(nbk)]
            + [jnp.where(_lane((rows, LANES)) <= trow, 0.0, NEG)], axis=1)
        for g in range(G_B):
            lg_s[g * grows:(g + 1) * grows, past:lp] = _dot_nt(
                qb[g * grows:(g + 1) * grows], _pad_rows(knew[:, g * DH:(g + 1) * DH], LANES))
        lg = _tail_bias(lg_s[...], cb_ref, tail_ref) + neg_mask
        p = jnp.exp(lg - jnp.max(lg, axis=1, keepdims=True))
        o = jnp.concatenate(
            [_dot_nt(p[g * grows:(g + 1) * grows, :past], vt_s[g * DH:(g + 1) * DH, :])
             + _dot(p[g * grows:(g + 1) * grows, past:], _pad_rows(vnew[:, g * DH:(g + 1) * DH], LANES))
             for g in range(G_B)], axis=0) / jnp.sum(p, axis=1, keepdims=True)
        o_ref[...] = _head_unstack(o, tdec)


def _mla_s_kernel(pt_ref, *refs, pp, tdec):
    pages = refs[:pp]
    ql_ref, qr_ref, rc_ref, wuv_ref, o_ref, m_s, l_s, acc_s = refs[pp:]
    s = pl.program_id(1)
    rows = H_C * tdec
    qlat, qrope = _mla_q_rows(ql_ref[...], qr_ref[...])
    scale = (D_NOPE + D_ROPE) ** -0.5

    @pl.when(s == 0)
    def _():
        m_s[...] = jnp.full(m_s.shape, -jnp.inf, F32)
        l_s[...] = jnp.zeros_like(l_s)
        acc_s[...] = jnp.zeros_like(acc_s)

    ckvt = _cat_pages(pages, 0, KV_LORA).astype(BF16)
    krt = _cat_pages(pages, KV_LORA, ROW_C).astype(BF16)
    sc = (jnp.dot(qlat, ckvt, preferred_element_type=F32) + jnp.dot(qrope, krt, preferred_element_type=F32)) * scale
    m_i, l_i, acc = _softmax_step(sc, lambda p: _dot_nt(p, ckvt), m_s[...], l_s[...], acc_s[...])
    m_s[...] = m_i
    l_s[...] = l_i
    acc_s[...] = acc

    @pl.when(s == pl.num_programs(1) - 1)
    def _():
        ckv = _pad_rows(rc_ref[:, 0:KV_LORA], LANES).astype(BF16)
        kr = _pad_rows(rc_ref[:, KV_LORA:ROW_C], LANES).astype(BF16)
        trow = jnp.concatenate([_row((tdec, LANES))] * H_C, axis=0)
        sn = (_dot_nt(qlat, ckv) + _dot_nt(qrope, kr)) * scale
        sn = jnp.where(_lane((rows, LANES)) <= trow, sn, NEG)
        _, l_f, acc_f = _softmax_step(sn, lambda p: _dot(p, ckv), m_s[...], l_s[...], acc_s[...])
        o_ref[...] = _mla_out(acc_f / l_f, wuv_ref, tdec)


def _bias_lookup(tab, bk):
    oh = jax.nn.one_hot(bk, N_BUCKETS, dtype=F32)
    return jnp.einsum('...b,bh->h...', oh, tab.astype(F32), precision=lax.Precision.HIGHEST)


def _bias_tables(tab, n_var, tq, chk, step):
    heads = tab.shape[1]
    i = jnp.arange(tq, dtype=I32)[:, None]
    j = jnp.arange(chk, dtype=I32)[None, :]
    out = []
    for r in range(n_var):
        dist = i - j + step * r
        bias = jnp.where(dist[None] < 0, NEG, _bias_lookup(tab, _t5_bucket(dist)))
        out.append(bias.reshape(heads * tq, chk))
    return jnp.stack(out)


def _tail_tables(tab, tdec):
    heads = tab.shape[1]
    t = jnp.arange(tdec, dtype=I32)[:, None]
    j = jnp.arange(2 * LANES, dtype=I32)[None, :]
    full = _bias_lookup(tab, _t5_bucket(t + LANES - j)).reshape(heads * tdec, 2 * LANES)
    far = tab[_t5_bucket(jnp.int32(1 << 20))].astype(F32)
    cb = jnp.repeat(far, tdec)[:, None]
    return cb, full - cb


def _pick_tile(n, cap, mult):
    best = mult
    for c in range(mult, cap + 1, mult):
        if n % c == 0:
            best = c
    return best


def kernel(x_prompt, x_sample, cache_a, cache_b, cache_c, state_d, page_table, p_prompt, p_sample, norm1_g, w_in, cq_norm_g, w_uq, ckv_norm_g, w_uk, w_uv, lb_logits, d_norm_g, t5_bias, w_branch, w_out, norm2_g, peer_wq, peer_subkeys, peer_u, peer_v, ple_gate, ple_proj, final_norm_g):
    bp, tp, d = x_prompt.shape
    bs, ts, _ = x_sample.shape
    depth = w_in.shape[0]
    npages = page_table.shape[1]
    page = cache_a.shape[2]
    past = npages * page
    n_p = bp * tp
    n_s = bs * ts
    n = n_p + n_s
    assert page == LANES and ts == 8 and tp % MOBA_BLOCK == 0 and past % MOBA_BLOCK == 0
    tm = _pick_tile(math.gcd(n_p, n_s), 256, 8)
    assert n_p % tm == 0 and n % LANES == 0
    tn_peer = _pick_tile(n, 640, LANES)
    tq = 128
    tq_a = 256
    chk = min(512, tp)
    pp = next(c for c in (64, 32, 16, 8, 2) if npages % c == 0)
    assert tp % chk == 0 and chk % tq_a == 0 and tp % tq_a == 0

    x = jnp.concatenate([x_prompt.reshape(n_p, d), x_sample.reshape(n_s, d)], axis=0)
    p_all = jnp.concatenate([p_prompt.reshape(depth, n_p, -1), p_sample.reshape(depth, n_s, -1)], axis=1)

    sm = jax.nn.softmax(lb_logits.astype(F32), axis=0)
    lb_all = jnp.maximum(jnp.cumsum(sm, axis=0) - sm[0], 0.0)
    zc = lambda k: jnp.zeros((depth, d, k), F32)
    o = np.cumsum((0,) + (H_A * DH, DH, DH, H_IDX * D_IDX, H_IDX, D_IDX, H_B * DH, G_B * DH, G_B * DH,
                          Q_LORA, KV_LORA, D_ROPE, 4 * H_D * DK_D))
    seg = lambda a, b: w_in[:, :, o[a]:o[b]]
    w_cat = jnp.concatenate([
        seg(12, 13),
        seg(0, 1),
        seg(1, 3), seg(5, 6), seg(4, 5), zc(256 - ROW_A - H_IDX),
        seg(6, 7), seg(7, 9),
        seg(9, 10), zc(256 - Q_LORA),
        seg(3, 4), seg(10, 11), seg(11, 12), zc(LANES - D_ROPE)], axis=2).astype(BF16)
    w_gate = w_in[:, :, o[13]:].astype(BF16)
    cqg = jnp.pad(cq_norm_g, ((0, 0), (0, 256 - Q_LORA)))[:, None, :]
    wuq3 = w_uq.reshape(depth, Q_LORA, H_C, D_NOPE + D_ROPE)
    wuq_p = jnp.concatenate([wuq3[..., :D_NOPE].reshape(depth, Q_LORA, H_C * D_NOPE),
                             wuq3[..., D_NOPE:].reshape(depth, Q_LORA, H_C * D_ROPE)], axis=2)
    wuq_p = jnp.pad(wuq_p, ((0, 0), (0, 256 - Q_LORA), (0, 0))).astype(BF16)
    wukbd = jnp.zeros((depth, H_C, D_NOPE, H_C, KV_LORA), F32)
    wuvp = jnp.zeros((depth, H_C, KV_LORA, H_C, D_V_C), F32)
    for h in range(H_C):
        wukbd = wukbd.at[:, h, :, h, :].set(jnp.swapaxes(w_uk[:, :, h, :], 1, 2))
        wuvp = wuvp.at[:, h, :, h, :].set(w_uv[:, :, h, :])
    wukbd = wukbd.reshape(depth, H_C * D_NOPE, H_C * KV_LORA).astype(BF16)
    wuvp = wuvp.reshape(depth, H_C, KV_LORA, H_C * D_V_C).astype(BF16)
    dgt = jnp.tile(d_norm_g, (1, H_D))[:, None, :]
    hd = lax.broadcasted_iota(I32, (H_D * DK_D, H_D * DK_D), 0) // DK_D
    ones_bd = (hd == hd.T).astype(BF16)
    wb = w_branch.astype(BF16)
    wo = w_out.astype(BF16)
    wq = peer_wq.astype(BF16)
    sk = peer_subkeys.reshape(depth, 2 * PEER_HEADS, N_KEYS, -1).astype(BF16)
    u_b = peer_u.astype(BF16)
    vt_b = jnp.swapaxes(peer_v, 1, 2).astype(BF16)
    pg = ple_gate.astype(BF16)
    ppj = ple_proj.astype(BF16)

    pos = jnp.concatenate([jnp.tile(jnp.arange(tp, dtype=I32), bp),
                           jnp.tile(past + jnp.arange(ts, dtype=I32), bs)])
    half = D_ROPE // 2
    inv = 1.0 / (ROPE_THETA ** (jnp.arange(half, dtype=F32) / half))
    ang = pos.astype(F32)[:, None] * inv
    cos_t = jnp.tile(jnp.cos(ang), (1, LANES // half))
    sgn = jnp.where((jnp.arange(LANES) % D_ROPE) < half, -1.0, 1.0).astype(F32)
    sin_t = jnp.tile(jnp.sin(ang), (1, LANES // half)) * sgn
    bt_a = _bias_tables(t5_bias[:, :H_A], chk // tq_a + 2, tq_a, chk, tq_a)
    bt_b = _bias_tables(t5_bias[:, H_A:], 3, MOBA_BLOCK, MOBA_BLOCK, MOBA_BLOCK)
    cb_a, tail_a = _tail_tables(t5_bias[:, :H_A], ts)
    cb_b, tail_b = _tail_tables(t5_bias[:, H_A:], ts)
    tri_c = (lax.broadcasted_iota(I32, (chk, chk), 0) < lax.broadcasted_iota(I32, (chk, chk), 1)).astype(BF16)
    tri_l = tri_c[:LANES, :LANES]
    topk_p = min(TOPK_A_MAX, tp // 4)
    topk_s = min(TOPK_A_MAX, (past + ts) // 4)
    w = H_D * DK_D
    st0_p = jnp.zeros((bp, w, w), F32)
    lp = past + LANES
    rows_dec = H_A * ts
    npool = cache_a.shape[1]
    cache_at = jnp.swapaxes(cache_a, 2, 3)
    cache_bt = jnp.transpose(cache_b, (0, 1, 3, 4, 5, 2)).reshape(depth, npool, 2 * G_B * DH, page)
    cache_ct = jnp.swapaxes(cache_c, 2, 3)
    nbl = -(-(past // MOBA_BLOCK) // LANES) * LANES

    rows_a, rows_b, rows_c, st_p, st_s = [], [], [], [], []
    y = None
    for li in range(depth):
        p = _proj_in(x, norm1_g[li][None, :], w_cat[li], tm)
        rc, ql, qr = _mla_prep(p, cos_t, sin_t, cqg[li], ckv_norm_g[li][None, :], wuq_p[li], wukbd[li], tm)
        rows_a.append(p[:, COL_AKV:COL_AKV + ROW_A])
        rows_b.append(p[:, COL_BKV:COL_BKV + 2 * G_B * DH])
        rows_c.append(rc)

        oa_p = _dsa_prompt(p, bt_a, tri_c, bp, tp, tq_a, chk, topk_p)
        ob_p = _moba_prompt(p, bt_b, bp, tp)
        oc_p = _mla_prompt(ql, qr, rc, wuvp[li], bp, tp, tq, chk)
        lbl = lb_all[li][None, :]
        od_p, stp = _hgrn(p, lbl, dgt[li], ones_bd, st0_p, bp, tp, 0, min(tp, 256), math.gcd(tp, CHUNK_D))
        st0_s = jnp.zeros((bs, H_D, DV_D, H_D, DK_D), F32)
        for h in range(H_D):
            st0_s = st0_s.at[:, h, :, h, :].set(jnp.swapaxes(state_d[li, :, h].astype(F32), 1, 2))
        st0_s = st0_s.reshape(bs, w, w)
        od_s, sts = _hgrn(p, lbl, dgt[li], ones_bd, st0_s, bs, ts, n_p, ts, math.gcd(ts, CHUNK_D))

        oa_s = _sample_call(
            functools.partial(_dsa_s_kernel, pp=pp, past=past, topk=topk_s, tdec=ts), "dsa_sample",
            cache_at, li, page_table, pp,
            [(p, 256, COL_AQ // 256), (p, 256, COL_AKV // 256), (p, 128, COL_AIQ // 128)],
            [cb_a, tail_a, tri_l], bs, ts, n_p,
            [pltpu.VMEM((ts, lp), I32), pltpu.VMEM((rows_dec, lp), F32), pltpu.VMEM((DH, past), BF16)])
        ob_s = _sample_call(
            functools.partial(_moba_s_kernel, pp=pp, past=past,
                              ntop=min(MOBA_TOPK, past // MOBA_BLOCK + 1), tdec=ts), "moba_sample",
            cache_bt, li, page_table, pp,
            [(p, 256, COL_BQ // 256), (p, 256, COL_BKV // 256)],
            [cb_b, tail_b], bs, ts, n_p,
            [pltpu.VMEM((rows_dec, lp), F32), pltpu.VMEM((G_B * DH, past), BF16),
             pltpu.VMEM((G_B * DH, nbl), F32)])
        oc_s = _sample_call(
            functools.partial(_mla_s_kernel, pp=pp, tdec=ts), "mla_sample",
            cache_ct, li, page_table, pp,
            [(ql, H_C * KV_LORA, 0), (qr, LANES, 0), (rc, ROW_C, 0)],
            [wuvp[li]], bs, ts, n_p,
            [pltpu.VMEM((rows_dec, 1), F32), pltpu.VMEM((rows_dec, 1), F32), pltpu.VMEM((rows_dec, KV_LORA), F32)])
        st_p.append(stp)
        st_s.append(sts)

        cat = lambda a, b: jnp.concatenate([a, b], axis=0)
        x1 = _merge(x, cat(oa_p, oa_s), cat(ob_p, ob_s), cat(oc_p, oc_s), cat(od_p, od_s),
                    norm1_g[li][None, :], w_gate[li], wb[li], wo[li], tm)
        g2 = norm2_g[li][None, :]
        th, s1, e0, e1 = _peer_select(x1, g2, wq[li], sk[li], _pick_tile(n, 256, LANES))
        po = _peer_dense(x1, g2, u_b[li], vt_b[li], th, s1, e0, e1, tn_peer, 8)
        x, y = _ple(x1, po, p_all[li], pg[li], ppj[li], final_norm_g[None, :], tm)

    def unstate(st, nb):
        s5 = jnp.stack(st).reshape(depth, nb, H_D, DV_D, H_D, DK_D)
        diag = jnp.stack([s5[:, :, h, :, h, :] for h in range(H_D)], axis=2)
        return jnp.swapaxes(diag, 3, 4)

    ra = jnp.stack(rows_a)
    rb = jnp.stack(rows_b)
    rcs = jnp.stack(rows_c)
    return (y[:n_p].reshape(bp, tp, d), y[n_p:].reshape(bs, ts, d),
            ra[:, :n_p].reshape(depth, bp, tp, ROW_A),
            rb[:, :n_p].reshape(depth, bp, tp, 2, G_B, DH),
            rcs[:, :n_p].reshape(depth, bp, tp, ROW_C),
            unstate(st_p, bp),
            ra[:, n_p:].reshape(depth, bs, ts, ROW_A),
            rb[:, n_p:].reshape(depth, bs, ts, 2, G_B, DH),
            rcs[:, n_p:].reshape(depth, bs, ts, ROW_C),
            unstate(st_s, bs))
```
